```python
import math
import jax, jax.numpy as jnp
from jax import lax
import numpy as np

D_MODEL = 1024
BATCH = 4
SEQ = 8192
DEPTH = 2
DEC_BATCH = 32
DEC_SEQ = 1
PAST_LEN = 16384
PAGE_SIZE = 128

MIX_WIDTH = D_MODEL
HEAD_DIM = 64
SB_HEADS = MIX_WIDTH // 2 // HEAD_DIM
SB_WIDTH = SB_HEADS * HEAD_DIM
SB_BLOCK = 128
SB_BIAS_INIT = -8.0
GLA_HEADS = MIX_WIDTH // 4 // HEAD_DIM
GLA_DV = HEAD_DIM
GLA_DK = HEAD_DIM // 2
GLA_WIDTH = GLA_HEADS * GLA_DV
GLA_KEY_WIDTH = GLA_HEADS * GLA_DK
GLA_RANK = 16
GLA_TAU = 16.0
GLA_CHUNK = 64
CM_GROUPS = 4
CM_WIDTH = MIX_WIDTH - SB_WIDTH - GLA_WIDTH
CM_GROUP_DIM = CM_WIDTH // CM_GROUPS
CM_CHUNK = 128
D_FF = ((8 * D_MODEL // 3 + 255) // 256) * 256
EPS = 1e-6
IN_SPLITS = (SB_WIDTH, SB_WIDTH, SB_WIDTH,
             GLA_KEY_WIDTH, GLA_KEY_WIDTH, GLA_WIDTH, GLA_RANK, GLA_WIDTH,
             CM_WIDTH, CM_WIDTH)
N_IN = sum(IN_SPLITS)

kernel_name = 'stick_gla_gmlp_hybrid_step'


def rms_norm(x, g):
    xf = x.astype(jnp.float32)
    y = xf * lax.rsqrt(jnp.mean(xf * xf, axis=-1, keepdims=True) + EPS)
    return (y * g.astype(jnp.float32)).astype(x.dtype)


def layer_norm(x, g, b):
    xf = x.astype(jnp.float32)
    xc = xf - jnp.mean(xf, axis=-1, keepdims=True)
    y = xc * lax.rsqrt(jnp.mean(xc * xc, axis=-1, keepdims=True) + EPS)
    return (y * g.astype(jnp.float32) + b.astype(jnp.float32)).astype(x.dtype)


def half_ffn(x, g, w_up, w_down):
    gate, up = jnp.split(rms_norm(x, g) @ w_up, 2, axis=-1)
    return x + 0.5 * ((jax.nn.silu(gate) * up) @ w_down)


def split_cols(t):
    out, start = [], 0
    for w in IN_SPLITS:
        out.append(t[..., start:start + w])
        start += w
    return out


def mixer_projections(h, w_in, q_g, k_g, w_g2, b_g, ln_g, ln_b):
    n, L, _ = h.shape
    sq, sk, sv, gq, gk, gv, glr, gout, cu, cv = split_cols(h @ w_in)
    sq = rms_norm(sq.reshape(n, L, SB_HEADS, HEAD_DIM), q_g)
    sk = rms_norm(sk.reshape(n, L, SB_HEADS, HEAD_DIM), k_g)
    sv = sv.reshape(n, L, SB_HEADS, HEAD_DIM)
    gq = gq.reshape(n, L, GLA_HEADS, GLA_DK) * GLA_DK ** -0.5
    gk = gk.reshape(n, L, GLA_HEADS, GLA_DK)
    gv = gv.reshape(n, L, GLA_HEADS, GLA_DV)
    log_a = jax.nn.log_sigmoid((glr @ w_g2 + b_g).astype(jnp.float32)) / GLA_TAU
    log_a = log_a.reshape(n, L, GLA_HEADS, GLA_DK)
    gout = gout.reshape(n, L, GLA_HEADS, GLA_DV)
    cu = jax.nn.gelu(cu).reshape(n, L, CM_GROUPS, CM_GROUP_DIM)
    cv = layer_norm(jax.nn.gelu(cv), ln_g, ln_b).reshape(n, L, CM_GROUPS, CM_GROUP_DIM)
    return sq, sk, sv, gq, gk, gv, log_a, gout, cu, cv


def sb_attend(q, k, v, bias, q_pos, k_pos):
    z = jnp.einsum('nqhd,nkhd->nhqk', q, k, preferred_element_type=jnp.float32) * HEAD_DIM ** -0.5
    z = z + bias.astype(jnp.float32)[None, :, None, None]
    mask = k_pos[None, :] < q_pos[:, None]
    log_keep = jnp.where(mask, jax.nn.log_sigmoid(-z), 0.0)
    later = lax.cumsum(log_keep, axis=3, reverse=True) - log_keep
    a = jnp.where(mask, jnp.exp(jax.nn.log_sigmoid(z) + later), 0.0)
    return jnp.einsum('nhqk,nkhd->nqhd', a.astype(v.dtype), v)


def sb_prompt(q, k, v, bias):
    n, L, H, d = q.shape
    nb = L // SB_BLOCK
    qb = q.reshape(n, nb, SB_BLOCK, H, d).transpose(1, 0, 2, 3, 4)
    k_pos = jnp.arange(L)

    def block(args):
        i, qi = args
        return sb_attend(qi, k, v, bias, i * SB_BLOCK + jnp.arange(SB_BLOCK), k_pos)

    out = lax.map(block, (jnp.arange(nb), qb))
    return out.transpose(1, 0, 2, 3, 4).reshape(n, L, H, d)


def gla_chunked(q, k, v, log_a, s0, chunk):
    n, L, H, dk = q.shape
    dv = v.shape[-1]
    nc = L // chunk
    r = lambda t: t.astype(jnp.float32).reshape(n, nc, chunk, H, t.shape[-1])
    q, k, v, la = r(q), r(k), r(v), r(log_a)
    b = jnp.cumsum(la, axis=2)
    b_last = b[:, :, -1:]
    q_t = q * jnp.exp(b)
    k_t = k * jnp.exp(-b)
    causal = jnp.tril(jnp.ones((chunk, chunk), bool))
    att = jnp.where(causal, jnp.einsum('ncthk,ncshk->nchts', q_t, k_t), 0.0)
    o = jnp.einsum('nchts,ncshv->ncthv', att, v)
    ds = jnp.einsum('ncshk,ncshv->nchkv', k * jnp.exp(b_last - b), v)
    decay = jnp.exp(b_last[:, :, 0])

    def step(s, xs):
        dec, d_s = xs
        return dec[..., None] * s + d_s, s

    s_final, s_in = lax.scan(step, s0.astype(jnp.float32),
                             (jnp.moveaxis(decay, 1, 0), jnp.moveaxis(ds, 1, 0)))
    o = o + jnp.einsum('ncthk,nchkv->ncthv', q_t, jnp.moveaxis(s_in, 0, 1))
    return o.reshape(n, L, H, dv), s_final


def chunk_mlp(u, v, w_s, b_s):
    n, L, G, dg = v.shape
    lp = -(-L // CM_CHUNK) * CM_CHUNK
    vp = jnp.pad(v, ((0, 0), (0, lp - L), (0, 0), (0, 0)))
    vc = vp.reshape(n, lp // CM_CHUNK, CM_CHUNK, G, dg)
    w = jnp.where(jnp.tril(jnp.ones((CM_CHUNK, CM_CHUNK), bool)), w_s, 0.0)
    mixed = jnp.einsum('gts,ncsgd->nctgd', w, vc) + b_s.T[None, None, :, :, None]
    return u * mixed.reshape(n, lp, G, dg)[:, :L]


def merge_heads(x, o_sb, o_gla, gla_gate, gla_norm_g, o_cm, w_out):
    n, L, _ = x.shape
    o_gla = rms_norm(o_gla.astype(x.dtype), gla_norm_g) * jax.nn.silu(gla_gate)
    o = jnp.concatenate([o_sb.reshape(n, L, SB_WIDTH), o_gla.reshape(n, L, GLA_WIDTH),
                         o_cm.reshape(n, L, CM_WIDTH)], axis=-1)
    return x + o @ w_out


def setup_inputs(seed: int = 0) -> dict:
    key = jax.random.key(seed)
    ks = jax.random.split(key, 32)
    n_pages = PAST_LEN // PAGE_SIZE
    n_used = DEC_BATCH * n_pages
    n_pool = n_used + (n_used + 3) // 4
    f32 = jnp.float32
    nrm = lambda k, shape, fan: jax.random.normal(k, shape, f32) * fan ** -0.5
    gain = lambda k, shape: 1.0 + 0.02 * jax.random.normal(k, shape, f32)
    page_table = jax.random.permutation(ks[5], n_pool)[:n_used].reshape(DEC_BATCH, n_pages).astype(jnp.int32)
    return {
        'x_prompt': jax.random.normal(ks[0], (BATCH, SEQ, D_MODEL), f32),
        'x_sample': jax.random.normal(ks[1], (DEC_BATCH, DEC_SEQ, D_MODEL), f32),
        'cache_k': jax.random.normal(ks[2], (DEPTH, n_pool, PAGE_SIZE, SB_HEADS, HEAD_DIM), f32),
        'cache_v': jax.random.normal(ks[3], (DEPTH, n_pool, PAGE_SIZE, SB_HEADS, HEAD_DIM), f32),
        'state_gla': 0.5 * jax.random.normal(ks[4], (DEPTH, DEC_BATCH, GLA_HEADS, GLA_DK, GLA_DV), f32),
        'page_table': page_table,
        'ffn1_norm_g': gain(ks[6], (DEPTH, D_MODEL)),
        'ffn1_w_up': nrm(ks[7], (DEPTH, D_MODEL, 2 * D_FF), D_MODEL),
        'ffn1_w_down': nrm(ks[8], (DEPTH, D_FF, D_MODEL), D_FF),
        'mix_norm_g': gain(ks[9], (DEPTH, D_MODEL)),
        'w_in': nrm(ks[10], (DEPTH, D_MODEL, N_IN), D_MODEL),
        'q_norm_g': gain(ks[11], (DEPTH, HEAD_DIM)),
        'k_norm_g': gain(ks[12], (DEPTH, HEAD_DIM)),
        'sb_logit_bias': SB_BIAS_INIT + 0.1 * jax.random.normal(ks[24], (DEPTH, SB_HEADS), f32),
        'gla_w_gate2': nrm(ks[13], (DEPTH, GLA_RANK, GLA_KEY_WIDTH), GLA_RANK),
        'gla_b_gate': 0.1 * jax.random.normal(ks[14], (DEPTH, GLA_KEY_WIDTH), f32),
        'gla_out_norm_g': gain(ks[15], (DEPTH, GLA_DV)),
        'cm_ln_g': gain(ks[16], (DEPTH, CM_WIDTH)),
        'cm_ln_b': 0.02 * jax.random.normal(ks[17], (DEPTH, CM_WIDTH), f32),
        'cm_w_spatial': nrm(ks[18], (DEPTH, CM_GROUPS, CM_CHUNK, CM_CHUNK), CM_CHUNK),
        'cm_b_spatial': gain(ks[19], (DEPTH, CM_GROUPS, CM_CHUNK)),
        'w_out': nrm(ks[20], (DEPTH, MIX_WIDTH, D_MODEL), MIX_WIDTH),
        'ffn2_norm_g': gain(ks[21], (DEPTH, D_MODEL)),
        'ffn2_w_up': nrm(ks[22], (DEPTH, D_MODEL, 2 * D_FF), D_MODEL),
        'ffn2_w_down': nrm(ks[23], (DEPTH, D_FF, D_MODEL), D_FF),
    }


def reference(x_prompt, x_sample, cache_k, cache_v, state_gla, page_table,
              ffn1_norm_g, ffn1_w_up, ffn1_w_down, mix_norm_g, w_in, q_norm_g, k_norm_g,
              sb_logit_bias, gla_w_gate2, gla_b_gate, gla_out_norm_g, cm_ln_g, cm_ln_b,
              cm_w_spatial, cm_b_spatial, w_out, ffn2_norm_g, ffn2_w_up, ffn2_w_down):
    xp, xs = x_prompt, x_sample
    nb_p, nb_s, n_dec = xp.shape[0], xs.shape[0], xs.shape[1]
    past = page_table.shape[1] * PAGE_SIZE
    kp_l, vp_l, sp_l, ks_l, vs_l, ss_l, cv_l = [], [], [], [], [], [], []
    for l in range(DEPTH):
        proj_w = (w_in[l], q_norm_g[l], k_norm_g[l], gla_w_gate2[l], gla_b_gate[l], cm_ln_g[l], cm_ln_b[l])
        xp = half_ffn(xp, ffn1_norm_g[l], ffn1_w_up[l], ffn1_w_down[l])
        sq, sk, sv, gq, gk, gv, la, gg, cu, cv = mixer_projections(rms_norm(xp, mix_norm_g[l]), *proj_w)
        o_sb = sb_prompt(sq, sk, sv, sb_logit_bias[l])
        s0 = jnp.zeros((nb_p, GLA_HEADS, GLA_DK, GLA_DV), jnp.float32)
        o_gla, s_p = gla_chunked(gq, gk, gv, la, s0, GLA_CHUNK)
        o_cm = chunk_mlp(cu, cv, cm_w_spatial[l], cm_b_spatial[l])
        xp = merge_heads(xp, o_sb, o_gla, gg, gla_out_norm_g[l], o_cm, w_out[l])
        xp = half_ffn(xp, ffn2_norm_g[l], ffn2_w_up[l], ffn2_w_down[l])
        kp_l.append(sk)
        vp_l.append(sv)
        sp_l.append(s_p)
        xs = half_ffn(xs, ffn1_norm_g[l], ffn1_w_up[l], ffn1_w_down[l])
        sq, sk, sv, gq, gk, gv, la, gg, cu, cv = mixer_projections(rms_norm(xs, mix_norm_g[l]), *proj_w)
        past_k = cache_k[l][page_table].reshape(nb_s, past, SB_HEADS, HEAD_DIM)
        past_v = cache_v[l][page_table].reshape(nb_s, past, SB_HEADS, HEAD_DIM)
        keys = jnp.concatenate([past_k.astype(sk.dtype), sk], axis=1)
        vals = jnp.concatenate([past_v.astype(sv.dtype), sv], axis=1)
        o_sb = sb_attend(sq, keys, vals, sb_logit_bias[l], past + jnp.arange(n_dec), jnp.arange(past + n_dec))
        o_gla, s_s = gla_chunked(gq, gk, gv, la, state_gla[l], n_dec)
        o_cm = chunk_mlp(cu, cv, cm_w_spatial[l], cm_b_spatial[l])
        xs = merge_heads(xs, o_sb, o_gla, gg, gla_out_norm_g[l], o_cm, w_out[l])
        xs = half_ffn(xs, ffn2_norm_g[l], ffn2_w_up[l], ffn2_w_down[l])
        ks_l.append(sk)
        vs_l.append(sv)
        ss_l.append(s_s)
        cv_l.append(cv)
    k_prompt = jnp.stack(kp_l)
    v_prompt = jnp.stack(vp_l)
    gla_state_prompt = jnp.stack(sp_l)
    k_sample = jnp.stack(ks_l)
    v_sample = jnp.stack(vs_l)
    gla_state_sample = jnp.stack(ss_l)
    cm_v_sample = jnp.stack(cv_l)
    return (xp, xs, k_prompt, v_prompt, gla_state_prompt, k_sample, v_sample, gla_state_sample, cm_v_sample)
```

```python
import functools

import jax
import jax.numpy as jnp
from jax import lax
from jax.experimental import pallas as pl
from jax.experimental.pallas import tpu as pltpu

F32 = jnp.float32
BF16 = jnp.bfloat16

D_MODEL = 1024
HEAD_DIM = 64
SB_HEADS = 8
SB_WIDTH = SB_HEADS * HEAD_DIM
GLA_HEADS = 4
GLA_DV = 64
GLA_DK = 32
GLA_WIDTH = GLA_HEADS * GLA_DV
GLA_KEY_WIDTH = GLA_HEADS * GLA_DK
GLA_RANK = 16
GLA_TAU = 16.0
GLA_CHUNK = 64
CM_GROUPS = 4
CM_WIDTH = 256
CM_GROUP_DIM = 64
CM_CHUNK = 128
D_FF = 2816
EPS = 1e-6
PAGE_SIZE = 128

LANES = 128
VMEM_LIMIT = 56 * 1024 * 1024

C_SQ, C_SK, C_SV = 0, 512, 1024
C_GQ, C_GK, C_GV, C_GG = 1536, 1664, 1792, 2048
C_CU, C_CV, C_LR = 2304, 2560, 2816
N_IN_PAD = 2944

NT = (((1,), (1,)), ((), ()))
TN = (((0,), (0,)), ((), ()))


def _params(*sem):
    return pltpu.CompilerParams(dimension_semantics=sem, vmem_limit_bytes=VMEM_LIMIT)


def _const_spec(shape):
    n = len(shape)
    return pl.BlockSpec(shape, lambda *_: (0,) * n)


def _softplus(z):
    return jnp.maximum(z, 0.0) + jnp.log1p(jnp.exp(-jnp.abs(z)))


def _split_bf16(x):
    hi = x.astype(BF16)
    lo = (x - hi.astype(F32)).astype(BF16)
    return hi, lo


def _dot(a, b):
    return jnp.dot(a, b, preferred_element_type=F32)


def _rms(x, g_row):
    return x * lax.rsqrt(jnp.mean(x * x, axis=-1, keepdims=True) + EPS) * g_row


def _ffn_kernel(x_ref, g_ref, wg_ref, wu_ref, wd_ref, o_ref, h_ref, acc_ref, *, nf):
    f = pl.program_id(1)

    @pl.when(f == 0)
    def _():
        h_ref[...] = _rms(x_ref[...], g_ref[...]).astype(BF16)
        acc_ref[...] = jnp.zeros_like(acc_ref)

    h = h_ref[...]
    gate = _dot(h, wg_ref[...])
    up = _dot(h, wu_ref[...])
    act = (gate * jax.nn.sigmoid(gate) * up).astype(BF16)
    acc_ref[...] += _dot(act, wd_ref[...])

    @pl.when(f == nf - 1)
    def _():
        o_ref[...] = x_ref[...] + 0.5 * acc_ref[...]


def _ffn(x, g_row, w_up, w_down, *, tm, tf):
    t, d = x.shape
    nf = D_FF // tf
    return pl.pallas_call(
        functools.partial(_ffn_kernel, nf=nf),
        grid=(t // tm, nf),
        in_specs=[
            pl.BlockSpec((tm, d), lambda i, f: (i, 0)),
            _const_spec((1, d)),
            pl.BlockSpec((d, tf), lambda i, f: (0, f)),
            pl.BlockSpec((d, tf), lambda i, f: (0, f + nf)),
            pl.BlockSpec((tf, d), lambda i, f: (f, 0)),
        ],
        out_specs=pl.BlockSpec((tm, d), lambda i, f: (i, 0)),
        out_shape=jax.ShapeDtypeStruct((t, d), F32),
        scratch_shapes=[pltpu.VMEM((tm, d), BF16), pltpu.VMEM((tm, d), F32)],
        compiler_params=_params("parallel", "arbitrary"),
        name="half_ffn",
    )(x, g_row, w_up, w_up, w_down)


def _mix_common(x_ref, g_ref, w_ref, qg_ref, kg_ref, wg2_ref, bg_ref, lng_ref, lnb_ref, hsum_ref):
    h = _rms(x_ref[...], g_ref[...]).astype(BF16)

    def proj(lo, width):
        return _dot(h, w_ref[:, lo:lo + width])

    def head_norm(t, gain_row):
        ms = _dot((t * t).astype(BF16), hsum_ref[...]) * (1.0 / HEAD_DIM)
        return t * lax.rsqrt(ms + EPS) * gain_row

    out = {}
    out["q"] = head_norm(proj(C_SQ, SB_WIDTH), qg_ref[...]) * (HEAD_DIM ** -0.5)
    out["k"] = head_norm(proj(C_SK, SB_WIDTH), kg_ref[...])
    out["v"] = proj(C_SV, SB_WIDTH)
    out["gq"] = proj(C_GQ, GLA_KEY_WIDTH) * (GLA_DK ** -0.5)
    out["gk"] = proj(C_GK, GLA_KEY_WIDTH)
    out["gv"] = proj(C_GV, GLA_WIDTH)
    gg = proj(C_GG, GLA_WIDTH)
    out["sg"] = gg * jax.nn.sigmoid(gg)
    lr_hi, lr_lo = _split_bf16(proj(C_LR, LANES))
    w_hi, w_lo = _split_bf16(wg2_ref[...])
    pre = _dot(lr_hi, w_hi) + _dot(lr_lo, w_hi) + _dot(lr_hi, w_lo) + bg_ref[...]
    out["la"] = -_softplus(-pre) * (1.0 / GLA_TAU)
    out["cu"] = jax.nn.gelu(proj(C_CU, CM_WIDTH))
    c = jax.nn.gelu(proj(C_CV, CM_WIDTH))
    c = c - jnp.mean(c, axis=-1, keepdims=True)
    out["cv"] = c * lax.rsqrt(jnp.mean(c * c, axis=-1, keepdims=True) + EPS) * lng_ref[...] + lnb_ref[...]
    return out


def _mix_prompt_kernel(x_ref, g_ref, w_ref, qg_ref, kg_ref, wg2_ref, bg_ref, lng_ref, lnb_ref, hsum_ref,
                       cum_ref, tot_ref, ws_ref, bs_ref,
                       q_o, kf_o, vf_o, kb_o, vb_o, qt_o, kt_o, kd_o, gv_o, eb_o, sg_o, ocm_o, *, tm):
    m = _mix_common(x_ref, g_ref, w_ref, qg_ref, kg_ref, wg2_ref, bg_ref, lng_ref, lnb_ref, hsum_ref)
    q_o[...] = m["q"].astype(BF16)
    kf_o[...] = m["k"]
    kb_o[...] = m["k"].astype(BF16)
    vf_o[...] = m["v"]
    vb_o[...] = m["v"].astype(BF16)
    gv_o[...] = m["gv"].astype(BF16)
    sg_o[...] = m["sg"]
    la_hi, la_lo = _split_bf16(m["la"])
    b = _dot(cum_ref[...], la_hi) + _dot(cum_ref[...], la_lo)
    b_last = _dot(tot_ref[...], la_hi) + _dot(tot_ref[...], la_lo)
    eb = jnp.exp(b)
    eb_o[...] = eb
    qt_o[...] = (m["gq"] * eb).astype(BF16)
    kt_o[...] = (m["gk"] * jnp.exp(-b)).astype(BF16)
    kd_o[...] = (m["gk"] * jnp.exp(b_last - b)).astype(BF16)
    row = lax.broadcasted_iota(jnp.int32, (CM_CHUNK, CM_CHUNK), 0)
    col = lax.broadcasted_iota(jnp.int32, (CM_CHUNK, CM_CHUNK), 1)
    group = lax.broadcasted_iota(jnp.int32, (1, CM_WIDTH), 1) // CM_GROUP_DIM
    w_tril = [jnp.where(row >= col, ws_ref[g], 0.0).astype(BF16) for g in range(CM_GROUPS)]
    cv = m["cv"].astype(BF16)
    for c in range(tm // CM_CHUNK):
        rows = slice(c * CM_CHUNK, (c + 1) * CM_CHUNK)
        mixed = bs_ref[...]
        for g in range(CM_GROUPS):
            mixed = mixed + _dot(w_tril[g], jnp.where(group == g, cv[rows], jnp.zeros_like(cv[rows])))
        ocm_o[rows, :] = (m["cu"][rows] * mixed).astype(BF16)


def _mix_sample_kernel(x_ref, g_ref, w_ref, qg_ref, kg_ref, wg2_ref, bg_ref, lng_ref, lnb_ref, hsum_ref,
                       w00_ref, b0_ref,
                       q_o, k_o, v_o, gq_o, gk_o, gv_o, a_o, sg_o, cv_o, ocm_o):
    m = _mix_common(x_ref, g_ref, w_ref, qg_ref, kg_ref, wg2_ref, bg_ref, lng_ref, lnb_ref, hsum_ref)
    q_o[...] = m["q"]
    k_o[...] = m["k"]
    v_o[...] = m["v"]
    gq_o[...] = m["gq"]
    gk_o[...] = m["gk"]
    gv_o[...] = m["gv"]
    a_o[...] = jnp.exp(m["la"])
    sg_o[...] = m["sg"]
    cv_o[...] = m["cv"]
    ocm_o[...] = (m["cu"] * (w00_ref[...] * m["cv"] + b0_ref[...])).astype(BF16)


def _mix_weight_specs():
    return [
        _const_spec((1, D_MODEL)),
        _const_spec((D_MODEL, N_IN_PAD)),
        _const_spec((1, SB_WIDTH)),
        _const_spec((1, SB_WIDTH)),
        _const_spec((LANES, GLA_KEY_WIDTH)),
        _const_spec((1, GLA_KEY_WIDTH)),
        _const_spec((1, CM_WIDTH)),
        _const_spec((1, CM_WIDTH)),
        _const_spec((SB_WIDTH, SB_WIDTH)),
    ]


def _mix_prompt(x, mw, cum, tot, w_s, bs_rows, *, tm):
    t = x.shape[0]
    row = lambda w: pl.BlockSpec((tm, w), lambda i: (i, 0))
    sds = lambda w, dt: jax.ShapeDtypeStruct((t, w), dt)
    widths = [(SB_WIDTH, BF16), (SB_WIDTH, F32), (SB_WIDTH, F32), (SB_WIDTH, BF16), (SB_WIDTH, BF16),
              (GLA_KEY_WIDTH, BF16), (GLA_KEY_WIDTH, BF16), (GLA_KEY_WIDTH, BF16), (GLA_WIDTH, BF16),
              (GLA_KEY_WIDTH, F32), (GLA_WIDTH, F32), (CM_WIDTH, BF16)]
    return pl.pallas_call(
        functools.partial(_mix_prompt_kernel, tm=tm),
        grid=(t // tm,),
        in_specs=[row(D_MODEL)] + _mix_weight_specs() + [
            _const_spec((tm, tm)), _const_spec((tm, tm)),
            _const_spec((CM_GROUPS, CM_CHUNK, CM_CHUNK)), _const_spec((CM_CHUNK, CM_WIDTH))],
        out_specs=[row(w) for w, _ in widths],
        out_shape=[sds(w, dt) for w, dt in widths],
        compiler_params=_params("parallel"),
        name="mix_prompt",
    )(x, *mw, cum, tot, w_s, bs_rows)


def _mix_sample(x, mw, w00_row, b0_row):
    t = x.shape[0]
    full = lambda w: _const_spec((t, w))
    widths = [(SB_WIDTH, F32), (SB_WIDTH, F32), (SB_WIDTH, F32), (GLA_KEY_WIDTH, F32), (GLA_KEY_WIDTH, F32),
              (GLA_WIDTH, F32), (GLA_KEY_WIDTH, F32), (GLA_WIDTH, F32), (CM_WIDTH, F32), (CM_WIDTH, BF16)]
    return pl.pallas_call(
        _mix_sample_kernel,
        grid=(1,),
        in_specs=[full(D_MODEL)] + _mix_weight_specs() + [_const_spec((1, CM_WIDTH)), _const_spec((1, CM_WIDTH))],
        out_specs=[full(w) for w, _ in widths],
        out_shape=[jax.ShapeDtypeStruct((t, w), dt) for w, dt in widths],
        compiler_params=_params("arbitrary"),
        name="mix_sample",
    )(x, *mw, w00_row, b0_row)


def _sb_prompt_kernel(bias_ref, q_ref, k_ref, v_ref, u_ref, o_ref, acc_ref, *, tq):
    hp = pl.program_id(1)
    qi = pl.program_id(2)
    lane_head = lax.broadcasted_iota(jnp.int32, (1, LANES), 1) // HEAD_DIM
    q = q_ref[...]
    q_heads = [jnp.where(lane_head == hh, q, jnp.zeros_like(q)) for hh in range(2)]
    biases = [bias_ref[2 * hp + hh] for hh in range(2)]
    row = lax.broadcasted_iota(jnp.int32, (tq, tq), 0)
    col = lax.broadcasted_iota(jnp.int32, (tq, tq), 1)
    acc_ref[...] = jnp.zeros_like(acc_ref)

    def tile(j, carries, diagonal):
        start = pl.multiple_of(j * tq, tq)
        kj = k_ref[pl.ds(start, tq), :]
        vj = v_ref[pl.ds(start, tq), :]
        new = []
        for hh in range(2):
            z = lax.dot_general(q_heads[hh], kj, NT, preferred_element_type=F32) + biases[hh]
            sp = _softplus(z)
            if diagonal:
                sp = jnp.where(col < row, sp, 0.0)
            hi, lo = _split_bf16(sp)
            later = _dot(hi, u_ref[...]) + _dot(lo, u_ref[...])
            a = jnp.exp(z - sp - later + carries[hh])
            if diagonal:
                a = jnp.where(col < row, a, 0.0)
            vh = jnp.where(lane_head == hh, vj, jnp.zeros_like(vj))
            acc_ref[...] += _dot(a.astype(BF16), vh)
            new.append(carries[hh] - jnp.sum(sp, axis=1, keepdims=True))
        return tuple(new)

    zero = jnp.zeros((tq, 1), F32)
    carries = tile(qi, (zero, zero), True)
    lax.fori_loop(0, qi, lambda i, c: tile(qi - 1 - i, c, False), carries)
    o_ref[...] = acc_ref[...].astype(BF16)


def _sb_prompt(q, k, v, bias, u, *, tq):
    n, L, _ = q.shape
    blk = lambda rows: (None, rows, LANES)
    return pl.pallas_call(
        functools.partial(_sb_prompt_kernel, tq=tq),
        grid=(n, SB_WIDTH // LANES, L // tq),
        in_specs=[
            pl.BlockSpec(memory_space=pltpu.SMEM),
            pl.BlockSpec(blk(tq), lambda b, h, i: (b, i, h)),
            pl.BlockSpec(blk(L), lambda b, h, i: (b, 0, h)),
            pl.BlockSpec(blk(L), lambda b, h, i: (b, 0, h)),
            _const_spec((tq, tq)),
        ],
        out_specs=pl.BlockSpec(blk(tq), lambda b, h, i: (b, i, h)),
        out_shape=jax.ShapeDtypeStruct((n, L, SB_WIDTH), BF16),
        scratch_shapes=[pltpu.VMEM((tq, LANES), F32)],
        compiler_params=_params("parallel", "parallel", "arbitrary"),
        name="sb_prompt",
    )(bias, q, k, v, u)


def _gla_prompt_kernel(qt_ref, kt_ref, kd_ref, v_ref, eb_ref, sg_ref, gn_ref, hsum_ref,
                       og_ref, st_ref, s_ref, o_ref, *, tc):
    @pl.when(pl.program_id(1) == 0)
    def _():
        s_ref[...] = jnp.zeros_like(s_ref)

    ck = GLA_CHUNK
    key_head = lax.broadcasted_iota(jnp.int32, (1, GLA_KEY_WIDTH), 1) // GLA_DK
    val_head = lax.broadcasted_iota(jnp.int32, (1, GLA_WIDTH), 1) // GLA_DV
    state_mask = (lax.broadcasted_iota(jnp.int32, (GLA_WIDTH, GLA_KEY_WIDTH), 0) // GLA_DV
                  == lax.broadcasted_iota(jnp.int32, (GLA_WIDTH, GLA_KEY_WIDTH), 1) // GLA_DK)
    causal = (lax.broadcasted_iota(jnp.int32, (ck, ck), 0) >= lax.broadcasted_iota(jnp.int32, (ck, ck), 1))
    for c in range(tc // ck):
        rows = slice(c * ck, (c + 1) * ck)
        qt, kt, kd, v = qt_ref[rows, :], kt_ref[rows, :], kd_ref[rows, :], v_ref[rows, :]
        att = []
        v_bd = []
        for hh in range(GLA_HEADS):
            qh = jnp.where(key_head == hh, qt, jnp.zeros_like(qt))
            s = lax.dot_general(qh, kt, NT, preferred_element_type=F32)
            att.append(jnp.where(causal, s, 0.0).astype(BF16))
            v_bd.append(jnp.where(val_head == hh, v, jnp.zeros_like(v)))
        st = s_ref[...]
        o = _dot(jnp.concatenate(att, axis=1), jnp.concatenate(v_bd, axis=0))
        o = o + lax.dot_general(qt, st.astype(BF16), NT, preferred_element_type=F32)
        o_ref[rows, :] = o
        ds = lax.dot_general(v, kd, TN, preferred_element_type=F32)
        decay = eb_ref[c * ck + ck - 1:c * ck + ck, :]
        s_ref[...] = decay * st + jnp.where(state_mask, ds, 0.0)
    o = o_ref[...]
    ms = _dot((o * o).astype(BF16), hsum_ref[...]) * (1.0 / GLA_DV)
    og_ref[...] = (o * lax.rsqrt(ms + EPS) * gn_ref[...] * sg_ref[...]).astype(BF16)
    st_ref[...] = s_ref[...]


def _gla_prompt(qt, kt, kd, gv, eb, sg, gn_row, hsum, *, n, L, tc):
    t = n * L
    nc = L // tc
    row = lambda w: pl.BlockSpec((tc, w), lambda b, i: (b * nc + i, 0))
    return pl.pallas_call(
        functools.partial(_gla_prompt_kernel, tc=tc),
        grid=(n, nc),
        in_specs=[row(GLA_KEY_WIDTH), row(GLA_KEY_WIDTH), row(GLA_KEY_WIDTH), row(GLA_WIDTH),
                  row(GLA_KEY_WIDTH), row(GLA_WIDTH), _const_spec((1, GLA_WIDTH)),
                  _const_spec((GLA_WIDTH, GLA_WIDTH))],
        out_specs=[row(GLA_WIDTH), pl.BlockSpec((None, GLA_WIDTH, GLA_KEY_WIDTH), lambda b, i: (b, 0, 0))],
        out_shape=[jax.ShapeDtypeStruct((t, GLA_WIDTH), BF16),
                   jax.ShapeDtypeStruct((n, GLA_WIDTH, GLA_KEY_WIDTH), F32)],
        scratch_shapes=[pltpu.VMEM((GLA_WIDTH, GLA_KEY_WIDTH), F32), pltpu.VMEM((tc, GLA_WIDTH), F32)],
        compiler_params=_params("parallel", "arbitrary"),
        name="gla_prompt",
    )(qt, kt, kd, gv, eb, sg, gn_row, hsum)


def _merge_kernel(x_ref, sb_ref, gla_ref, cm_ref, w_ref, o_ref):
    o = _dot(sb_ref[...], w_ref[0:SB_WIDTH, :])
    o = o + _dot(gla_ref[...], w_ref[SB_WIDTH:SB_WIDTH + GLA_WIDTH, :])
    o = o + _dot(cm_ref[...], w_ref[SB_WIDTH + GLA_WIDTH:, :])
    o_ref[...] = x_ref[...] + o


def _merge(x, o_sb, o_gla, o_cm, w_out, *, tm):
    t, d = x.shape
    row = lambda w: pl.BlockSpec((tm, w), lambda i: (i, 0))
    return pl.pallas_call(
        _merge_kernel,
        grid=(t // tm,),
        in_specs=[row(d), row(SB_WIDTH), row(GLA_WIDTH), row(CM_WIDTH), _const_spec((d, d))],
        out_specs=row(d),
        out_shape=jax.ShapeDtypeStruct((t, d), F32),
        compiler_params=_params("parallel"),
        name="merge_heads",
    )(x, o_sb, o_gla, o_cm, w_out)


def _sb_sample_kernel(pt_ref, q_ref, kown_ref, vown_ref, bias_ref, u_ref, *refs, pages, past):
    k_refs, v_refs = refs[:pages], refs[pages:2 * pages]
    o_ref, qb_ref, acc_ref, carry_ref = refs[2 * pages:]
    s = pl.program_id(1)
    bias = bias_ref[...]

    @pl.when(s == 0)
    def _():
        q = q_ref[...]
        qb_ref[...] = jnp.broadcast_to(q, qb_ref.shape)
        key_pos = past + 0 * lax.broadcasted_iota(jnp.int32, (SB_HEADS, 1), 0)
        valid = key_pos < past
        z = jnp.sum(q * kown_ref[...], axis=1) + bias
        sp = jnp.where(valid, _softplus(z), 0.0)
        a = jnp.where(valid, jnp.exp(z - sp), 0.0)
        lane = lax.broadcasted_iota(jnp.int32, acc_ref.shape, 2)
        own = jnp.broadcast_to(a[:, :, None] * vown_ref[...], acc_ref.shape)
        acc_ref[...] = jnp.where(lane == 0, own, 0.0)
        carry_ref[...] = -sp

    carry = carry_ref[...]
    qb = qb_ref[...]
    for p in reversed(range(pages)):
        z = jnp.sum(k_refs[p][...] * qb, axis=1) + bias
        sp = _softplus(z)
        hi = sp.astype(BF16).astype(F32)
        lo = (sp - hi).astype(BF16).astype(F32)
        later = _dot(hi, u_ref[...]) + _dot(lo, u_ref[...])
        a = jnp.exp(z - sp - later + carry)
        acc_ref[...] += a[:, None, :] * v_refs[p][...]
        carry = carry - jnp.sum(sp, axis=1, keepdims=True)
    carry_ref[...] = carry

    @pl.when(s == pl.num_programs(1) - 1)
    def _():
        o_ref[...] = jnp.sum(acc_ref[...], axis=2, keepdims=True)


def _sb_sample(page_table, q, k_own, v_own, bias_col, u, cache_kt, cache_vt, *, layer, pages):
    nb, n_pages = page_table.shape
    n_steps = n_pages // pages
    past = n_pages * PAGE_SIZE
    page_blk = (None, None, SB_HEADS, HEAD_DIM, PAGE_SIZE)

    def page_spec(p):
        return pl.BlockSpec(page_blk, lambda b, s, pt: (layer, pt[b, (n_steps - 1 - s) * pages + p], 0, 0, 0))

    per_b = pl.BlockSpec((None, SB_HEADS, HEAD_DIM, 1), lambda b, s, pt: (b, 0, 0, 0))
    grid_spec = pltpu.PrefetchScalarGridSpec(
        num_scalar_prefetch=1,
        grid=(nb, n_steps),
        in_specs=[per_b, per_b, per_b,
                  pl.BlockSpec((SB_HEADS, 1), lambda b, s, pt: (0, 0)),
                  pl.BlockSpec((PAGE_SIZE, PAGE_SIZE), lambda b, s, pt: (0, 0))]
                 + [page_spec(p) for p in range(pages)] * 2,
        out_specs=per_b,
        scratch_shapes=[pltpu.VMEM((SB_HEADS, HEAD_DIM, PAGE_SIZE), F32),
                        pltpu.VMEM((SB_HEADS, HEAD_DIM, PAGE_SIZE), F32),
                        pltpu.VMEM((SB_HEADS, 1), F32)],
    )
    return pl.pallas_call(
        functools.partial(_sb_sample_kernel, pages=pages, past=past),
        grid_spec=grid_spec,
        out_shape=jax.ShapeDtypeStruct((nb, SB_HEADS, HEAD_DIM, 1), F32),
        compiler_params=_params("parallel", "arbitrary"),
        name="sb_sample",
    )(page_table, q, k_own, v_own, bias_col, u, *([cache_kt] * pages), *([cache_vt] * pages))


def _gla_sample_kernel(a_ref, s_ref, k_ref, v_ref, q_ref, sg_ref, gn_ref, s_out, og_out):
    s_new = a_ref[...] * s_ref[...] + k_ref[...] * v_ref[...]
    s_out[...] = s_new
    o = jnp.sum(q_ref[...] * s_new, axis=2)
    og_out[...] = _rms(o, gn_ref[...]) * sg_ref[...]


def _gla_sample(a, s0, gk, gv, gq, sg, gn):
    nb = s0.shape[0]
    col = lambda t: t.reshape(nb, GLA_HEADS, GLA_DK, 1)
    args = (col(a), s0, col(gk), gv.reshape(nb, GLA_HEADS, 1, GLA_DV), col(gq),
            sg.reshape(nb, GLA_HEADS, GLA_DV), gn.reshape(1, 1, GLA_DV))
    return pl.pallas_call(
        _gla_sample_kernel,
        out_shape=[jax.ShapeDtypeStruct(s0.shape, F32), jax.ShapeDtypeStruct((nb, GLA_HEADS, GLA_DV), F32)],
        name="gla_sample",
    )(*args)


def _tile(t, target):
    return target if t % target == 0 else t


def kernel(x_prompt, x_sample, cache_k, cache_v, state_gla, page_table, ffn1_norm_g, ffn1_w_up, ffn1_w_down, mix_norm_g, w_in, q_norm_g, k_norm_g, sb_logit_bias, gla_w_gate2, gla_b_gate, gla_out_norm_g, cm_ln_g, cm_ln_b, cm_w_spatial, cm_b_spatial, w_out, ffn2_norm_g, ffn2_w_up, ffn2_w_down):
    nb_p, L, d = x_prompt.shape
    nb_s, n_dec, _ = x_sample.shape
    depth = w_in.shape[0]
    assert n_dec == 1 and d == D_MODEL and L % CM_CHUNK == 0
    tp = nb_p * L
    tm = _tile(L, 512)
    tq = _tile(L, 256)
    tf = D_FF // 2
    pages = 8 if page_table.shape[1] % 8 == 0 else 1

    idx = jnp.arange(tm)
    same_chunk = (idx[:, None] // GLA_CHUNK) == (idx[None, :] // GLA_CHUNK)
    cum = (same_chunk & (idx[None, :] <= idx[:, None])).astype(BF16)
    tot = same_chunk.astype(BF16)
    hsum512 = ((jnp.arange(SB_WIDTH)[:, None] // HEAD_DIM) == (jnp.arange(SB_WIDTH)[None, :] // HEAD_DIM)).astype(BF16)
    hsum256 = hsum512[:GLA_WIDTH, :GLA_WIDTH]
    iq = jnp.arange(tq)
    u_later = (iq[:, None] > iq[None, :]).astype(BF16)
    ip = jnp.arange(PAGE_SIZE)
    u_page = (ip[:, None] > ip[None, :]).astype(F32)
    cache_kt = jnp.transpose(cache_k, (0, 1, 3, 4, 2))
    cache_vt = jnp.transpose(cache_v, (0, 1, 3, 4, 2))

    xp = x_prompt.reshape(tp, d)
    xs = x_sample.reshape(nb_s * n_dec, d)
    row = lambda v: v.reshape(1, -1).astype(F32)
    outs = {k: [] for k in ("kp", "vp", "sp", "ks", "vs", "ss", "cv")}
    for l in range(depth):
        w = w_in[l]
        w_r = jnp.concatenate([w[:, :C_GG], w[:, C_GG + GLA_RANK:], w[:, C_GG:C_GG + GLA_RANK],
                               jnp.zeros((d, N_IN_PAD - w.shape[1]), w.dtype)], axis=1).astype(BF16)
        wg2 = jnp.concatenate([gla_w_gate2[l], jnp.zeros((LANES - GLA_RANK, GLA_KEY_WIDTH), F32)], axis=0)
        mw = (row(mix_norm_g[l]), w_r, row(jnp.tile(q_norm_g[l], SB_HEADS)), row(jnp.tile(k_norm_g[l], SB_HEADS)),
              wg2, row(gla_b_gate[l]), row(cm_ln_g[l]), row(cm_ln_b[l]), hsum512)
        f1 = (row(ffn1_norm_g[l]), ffn1_w_up[l].astype(BF16), ffn1_w_down[l].astype(BF16))
        f2 = (row(ffn2_norm_g[l]), ffn2_w_up[l].astype(BF16), ffn2_w_down[l].astype(BF16))
        wo = w_out[l].astype(BF16)
        gn_row = row(jnp.tile(gla_out_norm_g[l], GLA_HEADS))
        bs_rows = jnp.repeat(cm_b_spatial[l].T, CM_GROUP_DIM, axis=1)

        xp = _ffn(xp, *f1, tm=tm, tf=tf)
        (q, kf, vf, kb, vb, qt, kt, kd, gv, eb, sg, ocm) = _mix_prompt(xp, mw, cum, tot, cm_w_spatial[l], bs_rows, tm=tm)
        r3 = lambda t: t.reshape(nb_p, L, SB_WIDTH)
        o_sb = _sb_prompt(r3(q), r3(kb), r3(vb), sb_logit_bias[l], u_later, tq=tq).reshape(tp, SB_WIDTH)
        o_gla, st = _gla_prompt(qt, kt, kd, gv, eb, sg, gn_row, hsum256, n=nb_p, L=L, tc=tm)
        xp = _merge(xp, o_sb, o_gla, ocm, wo, tm=tm)
        xp = _ffn(xp, *f2, tm=tm, tf=tf)
        outs["kp"].append(kf.reshape(nb_p, L, SB_HEADS, HEAD_DIM))
        outs["vp"].append(vf.reshape(nb_p, L, SB_HEADS, HEAD_DIM))
        st5 = st.reshape(nb_p, GLA_HEADS, GLA_DV, GLA_HEADS, GLA_DK)
        outs["sp"].append(jnp.stack([st5[:, h, :, h, :] for h in range(GLA_HEADS)], axis=1).transpose(0, 1, 3, 2))

        ts = xs.shape[0]
        xs = _ffn(xs, *f1, tm=ts, tf=tf)
        w00_row = row(jnp.repeat(cm_w_spatial[l][:, 0, 0], CM_GROUP_DIM))
        b0_row = row(jnp.repeat(cm_b_spatial[l][:, 0], CM_GROUP_DIM))
        (q, k, v, gq, gk, gv, a, sg, cv, ocm) = _mix_sample(xs, mw, w00_row, b0_row)
        col4 = lambda t: t.reshape(nb_s, SB_HEADS, HEAD_DIM, 1)
        o_sb = _sb_sample(page_table, col4(q), col4(k), col4(v), sb_logit_bias[l].reshape(SB_HEADS, 1), u_page,
                          cache_kt, cache_vt, layer=l, pages=pages).reshape(ts, SB_WIDTH)
        s_new, og = _gla_sample(a, state_gla[l], gk, gv, gq, sg, gla_out_norm_g[l])
        xs = _merge(xs, o_sb.astype(BF16), og.reshape(ts, GLA_WIDTH).astype(BF16), ocm, wo, tm=ts)
        xs = _ffn(xs, *f2, tm=ts, tf=tf)
        outs["ks"].append(k.reshape(nb_s, n_dec, SB_HEADS, HEAD_DIM))
        outs["vs"].append(v.reshape(nb_s, n_dec, SB_HEADS, HEAD_DIM))
        outs["ss"].append(s_new)
        outs["cv"].append(cv.reshape(nb_s, n_dec, CM_GROUPS, CM_GROUP_DIM))

    st = lambda k: jnp.stack(outs[k])
    return (xp.reshape(nb_p, L, d), xs.reshape(nb_s, n_dec, d), st("kp"), st("vp"), st("sp"),
            st("ks"), st("vs"), st("ss"), st("cv"))
```

```python
import functools

import jax
import jax.numpy as jnp
from jax import lax
from jax.experimental import pallas as pl
from jax.experimental.pallas import tpu as pltpu

F32 = jnp.float32
BF16 = jnp.bfloat16

D_MODEL = 1024
HEAD_DIM = 64
SB_HEADS = 8
SB_WIDTH = SB_HEADS * HEAD_DIM
GLA_HEADS = 4
GLA_DV = 64
GLA_DK = 32
GLA_WIDTH = GLA_HEADS * GLA_DV
GLA_KEY_WIDTH = GLA_HEADS * GLA_DK
GLA_RANK = 16
GLA_TAU = 16.0
GLA_CHUNK = 64
CM_GROUPS = 4
CM_WIDTH = 256
CM_GROUP_DIM = 64
CM_CHUNK = 128
D_FF = 2816
EPS = 1e-6
PAGE_SIZE = 128

LOG2E = 1.4426950408889634
MASKED_LOG = -1e30
LANES = 128
VMEM_LIMIT = 56 * 1024 * 1024

C_SQ, C_SK, C_SV = 0, 512, 1024
C_GQ, C_GK, C_GV, C_GG = 1536, 1664, 1792, 2048
C_CU, C_CV, C_LR = 2304, 2560, 2816
N_IN_PAD = 2944

NT = (((1,), (1,)), ((), ()))
TN = (((0,), (0,)), ((), ()))


def _params(*sem):
    return pltpu.CompilerParams(dimension_semantics=sem, vmem_limit_bytes=VMEM_LIMIT)


def _const_spec(shape):
    n = len(shape)
    return pl.BlockSpec(shape, lambda *_: (0,) * n)


def _softplus(z):
    return jnp.maximum(z, 0.0) + jnp.log1p(jnp.exp(-jnp.abs(z)))


def _softplus2(x):
    neg_abs = pltpu.bitcast(pltpu.bitcast(x, jnp.uint32) | jnp.uint32(0x80000000), F32)
    return jnp.maximum(x, 0.0) + jnp.log2(1.0 + jnp.exp2(neg_abs))


def _split_bf16(x):
    hi = x.astype(BF16)
    lo = (x - hi.astype(F32)).astype(BF16)
    return hi, lo


def _dot(a, b):
    return jnp.dot(a, b, preferred_element_type=F32)


def _rms(x, g_row):
    return x * lax.rsqrt(jnp.mean(x * x, axis=-1, keepdims=True) + EPS) * g_row


def _ffn_kernel(x_ref, g_ref, wg_ref, wu_ref, wd_ref, o_ref, h_ref, acc_ref, *, nf):
    f = pl.program_id(1)

    @pl.when(f == 0)
    def _():
        h_ref[...] = _rms(x_ref[...], g_ref[...]).astype(BF16)
        acc_ref[...] = jnp.zeros_like(acc_ref)

    h = h_ref[...]
    gate = _dot(h, wg_ref[...])
    up = _dot(h, wu_ref[...])
    act = (gate * jax.nn.sigmoid(gate) * up).astype(BF16)
    acc_ref[...] += _dot(act, wd_ref[...])

    @pl.when(f == nf - 1)
    def _():
        o_ref[...] = x_ref[...] + 0.5 * acc_ref[...]


def _ffn(x, g_row, w_up, w_down, *, tm, tf):
    t, d = x.shape
    nf = D_FF // tf
    return pl.pallas_call(
        functools.partial(_ffn_kernel, nf=nf),
        grid=(t // tm, nf),
        in_specs=[
            pl.BlockSpec((tm, d), lambda i, f: (i, 0)),
            _const_spec((1, d)),
            pl.BlockSpec((d, tf), lambda i, f: (0, f)),
            pl.BlockSpec((d, tf), lambda i, f: (0, f + nf)),
            pl.BlockSpec((tf, d), lambda i, f: (f, 0)),
        ],
        out_specs=pl.BlockSpec((tm, d), lambda i, f: (i, 0)),
        out_shape=jax.ShapeDtypeStruct((t, d), F32),
        scratch_shapes=[pltpu.VMEM((tm, d), BF16), pltpu.VMEM((tm, d), F32)],
        compiler_params=_params("parallel", "arbitrary"),
        name="half_ffn",
    )(x, g_row, w_up, w_up, w_down)


def _mix_common(x_ref, g_ref, w_ref, qg_ref, kg_ref, wg2_ref, bg_ref, lng_ref, lnb_ref, hsum_ref):
    h = _rms(x_ref[...], g_ref[...]).astype(BF16)

    def proj(lo, width):
        return _dot(h, w_ref[:, lo:lo + width])

    def head_norm(t, gain_row):
        ms = _dot((t * t).astype(BF16), hsum_ref[...]) * (1.0 / HEAD_DIM)
        return t * lax.rsqrt(ms + EPS) * gain_row

    out = {}
    out["q"] = head_norm(proj(C_SQ, SB_WIDTH), qg_ref[...]) * (HEAD_DIM ** -0.5)
    out["k"] = head_norm(proj(C_SK, SB_WIDTH), kg_ref[...])
    out["v"] = proj(C_SV, SB_WIDTH)
    out["gq"] = proj(C_GQ, GLA_KEY_WIDTH) * (GLA_DK ** -0.5)
    out["gk"] = proj(C_GK, GLA_KEY_WIDTH)
    out["gv"] = proj(C_GV, GLA_WIDTH)
    gg = proj(C_GG, GLA_WIDTH)
    out["sg"] = gg * jax.nn.sigmoid(gg)
    lr_hi, lr_lo = _split_bf16(proj(C_LR, LANES))
    w_hi, w_lo = _split_bf16(wg2_ref[...])
    pre = _dot(lr_hi, w_hi) + _dot(lr_lo, w_hi) + _dot(lr_hi, w_lo) + bg_ref[...]
    out["la"] = -_softplus(-pre) * (1.0 / GLA_TAU)
    out["cu"] = jax.nn.gelu(proj(C_CU, CM_WIDTH))
    c = jax.nn.gelu(proj(C_CV, CM_WIDTH))
    c = c - jnp.mean(c, axis=-1, keepdims=True)
    out["cv"] = c * lax.rsqrt(jnp.mean(c * c, axis=-1, keepdims=True) + EPS) * lng_ref[...] + lnb_ref[...]
    return out


def _mix_prompt_kernel(x_ref, g_ref, w_ref, qg_ref, kg_ref, wg2_ref, bg_ref, lng_ref, lnb_ref, hsum_ref,
                       cum_ref, tot_ref, ws_ref, bs_ref,
                       q_o, kf_o, vf_o, kb_o, vb_o, qt_o, kt_o, kd_o, gv_o, eb_o, sg_o, ocm_o, *, tm):
    m = _mix_common(x_ref, g_ref, w_ref, qg_ref, kg_ref, wg2_ref, bg_ref, lng_ref, lnb_ref, hsum_ref)
    q_o[...] = (m["q"] * LOG2E).astype(BF16)
    kf_o[...] = m["k"]
    kb_o[...] = m["k"].astype(BF16)
    vf_o[...] = m["v"]
    vb_o[...] = m["v"].astype(BF16)
    gv_o[...] = m["gv"].astype(BF16)
    sg_o[...] = m["sg"]
    la_hi, la_lo = _split_bf16(m["la"])
    b = _dot(cum_ref[...], la_hi) + _dot(cum_ref[...], la_lo)
    b_last = _dot(tot_ref[...], la_hi) + _dot(tot_ref[...], la_lo)
    eb = jnp.exp(b)
    eb_o[...] = eb
    qt_o[...] = (m["gq"] * eb).astype(BF16)
    kt_o[...] = (m["gk"] * jnp.exp(-b)).astype(BF16)
    kd_o[...] = (m["gk"] * jnp.exp(b_last - b)).astype(BF16)
    row = lax.broadcasted_iota(jnp.int32, (CM_CHUNK, CM_CHUNK), 0)
    col = lax.broadcasted_iota(jnp.int32, (CM_CHUNK, CM_CHUNK), 1)
    group = lax.broadcasted_iota(jnp.int32, (1, CM_WIDTH), 1) // CM_GROUP_DIM
    w_tril = [jnp.where(row >= col, ws_ref[g], 0.0).astype(BF16) for g in range(CM_GROUPS)]
    cv = m["cv"].astype(BF16)
    for c in range(tm // CM_CHUNK):
        rows = slice(c * CM_CHUNK, (c + 1) * CM_CHUNK)
        mixed = bs_ref[...]
        for g in range(CM_GROUPS):
            mixed = mixed + _dot(w_tril[g], jnp.where(group == g, cv[rows], jnp.zeros_like(cv[rows])))
        ocm_o[rows, :] = (m["cu"][rows] * mixed).astype(BF16)


def _mix_sample_kernel(x_ref, g_ref, w_ref, qg_ref, kg_ref, wg2_ref, bg_ref, lng_ref, lnb_ref, hsum_ref,
                       w00_ref, b0_ref,
                       q_o, k_o, v_o, gq_o, gk_o, gv_o, a_o, sg_o, cv_o, ocm_o):
    m = _mix_common(x_ref, g_ref, w_ref, qg_ref, kg_ref, wg2_ref, bg_ref, lng_ref, lnb_ref, hsum_ref)
    q_o[...] = m["q"]
    k_o[...] = m["k"]
    v_o[...] = m["v"]
    gq_o[...] = m["gq"]
    gk_o[...] = m["gk"]
    gv_o[...] = m["gv"]
    a_o[...] = jnp.exp(m["la"])
    sg_o[...] = m["sg"]
    cv_o[...] = m["cv"]
    ocm_o[...] = (m["cu"] * (w00_ref[...] * m["cv"] + b0_ref[...])).astype(BF16)


def _mix_weight_specs():
    return [
        _const_spec((1, D_MODEL)),
        _const_spec((D_MODEL, N_IN_PAD)),
        _const_spec((1, SB_WIDTH)),
        _const_spec((1, SB_WIDTH)),
        _const_spec((LANES, GLA_KEY_WIDTH)),
        _const_spec((1, GLA_KEY_WIDTH)),
        _const_spec((1, CM_WIDTH)),
        _const_spec((1, CM_WIDTH)),
        _const_spec((SB_WIDTH, SB_WIDTH)),
    ]


def _mix_prompt(x, mw, cum, tot, w_s, bs_rows, *, tm):
    t = x.shape[0]
    row = lambda w: pl.BlockSpec((tm, w), lambda i: (i, 0))
    sds = lambda w, dt: jax.ShapeDtypeStruct((t, w), dt)
    widths = [(SB_WIDTH, BF16), (SB_WIDTH, F32), (SB_WIDTH, F32), (SB_WIDTH, BF16), (SB_WIDTH, BF16),
              (GLA_KEY_WIDTH, BF16), (GLA_KEY_WIDTH, BF16), (GLA_KEY_WIDTH, BF16), (GLA_WIDTH, BF16),
              (GLA_KEY_WIDTH, F32), (GLA_WIDTH, F32), (CM_WIDTH, BF16)]
    return pl.pallas_call(
        functools.partial(_mix_prompt_kernel, tm=tm),
        grid=(t // tm,),
        in_specs=[row(D_MODEL)] + _mix_weight_specs() + [
            _const_spec((tm, tm)), _const_spec((tm, tm)),
            _const_spec((CM_GROUPS, CM_CHUNK, CM_CHUNK)), _const_spec((CM_CHUNK, CM_WIDTH))],
        out_specs=[row(w) for w, _ in widths],
        out_shape=[sds(w, dt) for w, dt in widths],
        compiler_params=_params("parallel"),
        name="mix_prompt",
    )(x, *mw, cum, tot, w_s, bs_rows)


def _mix_sample(x, mw, w00_row, b0_row):
    t = x.shape[0]
    full = lambda w: _const_spec((t, w))
    widths = [(SB_WIDTH, F32), (SB_WIDTH, F32), (SB_WIDTH, F32), (GLA_KEY_WIDTH, F32), (GLA_KEY_WIDTH, F32),
              (GLA_WIDTH, F32), (GLA_KEY_WIDTH, F32), (GLA_WIDTH, F32), (CM_WIDTH, F32), (CM_WIDTH, BF16)]
    return pl.pallas_call(
        _mix_sample_kernel,
        grid=(1,),
        in_specs=[full(D_MODEL)] + _mix_weight_specs() + [_const_spec((1, CM_WIDTH)), _const_spec((1, CM_WIDTH))],
        out_specs=[full(w) for w, _ in widths],
        out_shape=[jax.ShapeDtypeStruct((t, w), dt) for w, dt in widths],
        compiler_params=_params("arbitrary"),
        name="mix_sample",
    )(x, *mw, w00_row, b0_row)


def _sb_prompt_kernel(bias_ref, q_ref, k_ref, v_ref, u_ref, o_ref, acc_ref, za_ref, zb_ref, w_ref, c_ref, r_ref,
                      *, tq):
    hp = pl.program_id(1)
    qi = pl.program_id(2)
    lane_head = lax.broadcasted_iota(jnp.int32, (1, LANES), 1) // HEAD_DIM
    q = q_ref[...]
    q2 = jnp.concatenate([jnp.where(lane_head == hh, q, jnp.zeros_like(q)) for hh in range(2)], axis=0)
    biases = [bias_ref[2 * hp + hh] * LOG2E for hh in range(2)]
    row = lax.broadcasted_iota(jnp.int32, (tq, tq), 0)
    col = lax.broadcasted_iota(jnp.int32, (tq, tq), 1)
    heads = [slice(hh * tq, (hh + 1) * tq) for hh in range(2)]

    def scores(j):
        kj = k_ref[pl.ds(pl.multiple_of(j * tq, tq), tq), :]
        return lax.dot_general(q2, kj, NT, preferred_element_type=F32)

    def log_terms(z_ref, diagonal):
        for hh in range(2):
            z = z_ref[heads[hh], :] + biases[hh]
            sp = _softplus2(z)
            if diagonal:
                sp = jnp.where(col < row, sp, 0.0)
            hi = pltpu.bitcast(pltpu.bitcast(sp, jnp.uint32) & jnp.uint32(0xFFFF0000), F32)
            hl = jnp.concatenate([hi.astype(BF16), (sp - hi).astype(BF16)], axis=1)
            w = z - sp - _dot(hl, u_ref[...])
            if diagonal:
                w = jnp.where(col < row, w, MASKED_LOG)
            w_ref[heads[hh], :] = w
            r_ref[heads[hh], :] = jnp.broadcast_to(jnp.sum(sp, axis=1, keepdims=True), (tq, LANES))

    def attend(j):
        vj = v_ref[pl.ds(pl.multiple_of(j * tq, tq), tq), :]
        a = []
        for hh in range(2):
            c = c_ref[heads[hh], :]
            a.append(jnp.exp2(w_ref[heads[hh], :] + jnp.concatenate([c] * (tq // LANES), axis=1)).astype(BF16))
        v2 = jnp.concatenate([jnp.where(lane_head == hh, vj, jnp.zeros_like(vj)) for hh in range(2)], axis=0)
        acc_ref[...] += _dot(jnp.concatenate(a, axis=1), v2)

    def step(n, z_cur, z_nxt):
        attend(qi - n + 1)
        c_ref[...] = c_ref[...] - r_ref[...]
        z_nxt[...] = scores(jnp.maximum(qi - n - 1, 0))
        log_terms(z_cur, False)

    acc_ref[...] = jnp.zeros_like(acc_ref)
    c_ref[...] = jnp.zeros_like(c_ref)
    za_ref[...] = scores(qi)
    log_terms(za_ref, True)
    zb_ref[...] = scores(jnp.maximum(qi - 1, 0))

    def body(i, _):
        step(2 * i + 1, zb_ref, za_ref)
        step(2 * i + 2, za_ref, zb_ref)
        return 0

    lax.fori_loop(0, (qi + 1) // 2, body, 0)

    @pl.when(qi % 2 == 0)
    def _():
        attend(0)

    o_ref[...] = acc_ref[...].astype(BF16)


def _sb_prompt(q, k, v, bias, u, *, tq):
    n, L, _ = q.shape
    blk = lambda rows: (None, rows, LANES)
    return pl.pallas_call(
        functools.partial(_sb_prompt_kernel, tq=tq),
        grid=(n, SB_WIDTH // LANES, L // tq),
        in_specs=[
            pl.BlockSpec(memory_space=pltpu.SMEM),
            pl.BlockSpec(blk(tq), lambda b, h, i: (b, i, h)),
            pl.BlockSpec(blk(L), lambda b, h, i: (b, 0, h)),
            pl.BlockSpec(blk(L), lambda b, h, i: (b, 0, h)),
            _const_spec((2 * tq, tq)),
        ],
        out_specs=pl.BlockSpec(blk(tq), lambda b, h, i: (b, i, h)),
        out_shape=jax.ShapeDtypeStruct((n, L, SB_WIDTH), BF16),
        scratch_shapes=[pltpu.VMEM((tq, LANES), F32)] + [pltpu.VMEM((2 * tq, tq), F32)] * 3
                       + [pltpu.VMEM((2 * tq, LANES), F32)] * 2,
        compiler_params=_params("parallel", "parallel", "arbitrary"),
        name="sb_prompt",
    )(bias, q, k, v, u)


def _gla_prompt_kernel(qt_ref, kt_ref, kd_ref, v_ref, eb_ref, sg_ref, gn_ref, hsum_ref,
                       og_ref, st_ref, s_ref, o_ref, *, tc):
    @pl.when(pl.program_id(1) == 0)
    def _():
        s_ref[...] = jnp.zeros_like(s_ref)

    ck = GLA_CHUNK
    key_head = lax.broadcasted_iota(jnp.int32, (1, GLA_KEY_WIDTH), 1) // GLA_DK
    val_head = lax.broadcasted_iota(jnp.int32, (1, GLA_WIDTH), 1) // GLA_DV
    state_mask = (lax.broadcasted_iota(jnp.int32, (GLA_WIDTH, GLA_KEY_WIDTH), 0) // GLA_DV
                  == lax.broadcasted_iota(jnp.int32, (GLA_WIDTH, GLA_KEY_WIDTH), 1) // GLA_DK)
    causal = (lax.broadcasted_iota(jnp.int32, (ck, ck), 0) >= lax.broadcasted_iota(jnp.int32, (ck, ck), 1))
    for c in range(tc // ck):
        rows = slice(c * ck, (c + 1) * ck)
        qt, kt, kd, v = qt_ref[rows, :], kt_ref[rows, :], kd_ref[rows, :], v_ref[rows, :]
        att = []
        v_bd = []
        for hh in range(GLA_HEADS):
            qh = jnp.where(key_head == hh, qt, jnp.zeros_like(qt))
            s = lax.dot_general(qh, kt, NT, preferred_element_type=F32)
            att.append(jnp.where(causal, s, 0.0).astype(BF16))
            v_bd.append(jnp.where(val_head == hh, v, jnp.zeros_like(v)))
        st = s_ref[...]
        o = _dot(jnp.concatenate(att, axis=1), jnp.concatenate(v_bd, axis=0))
        o = o + lax.dot_general(qt, st.astype(BF16), NT, preferred_element_type=F32)
        o_ref[rows, :] = o
        ds = lax.dot_general(v, kd, TN, preferred_element_type=F32)
        decay = eb_ref[c * ck + ck - 1:c * ck + ck, :]
        s_ref[...] = decay * st + jnp.where(state_mask, ds, 0.0)
    o = o_ref[...]
    ms = _dot((o * o).astype(BF16), hsum_ref[...]) * (1.0 / GLA_DV)
    og_ref[...] = (o * lax.rsqrt(ms + EPS) * gn_ref[...] * sg_ref[...]).astype(BF16)
    st_ref[...] = s_ref[...]


def _gla_prompt(qt, kt, kd, gv, eb, sg, gn_row, hsum, *, n, L, tc):
    t = n * L
    nc = L // tc
    row = lambda w: pl.BlockSpec((tc, w), lambda b, i: (b * nc + i, 0))
    return pl.pallas_call(
        functools.partial(_gla_prompt_kernel, tc=tc),
        grid=(n, nc),
        in_specs=[row(GLA_KEY_WIDTH), row(GLA_KEY_WIDTH), row(GLA_KEY_WIDTH), row(GLA_WIDTH),
                  row(GLA_KEY_WIDTH), row(GLA_WIDTH), _const_spec((1, GLA_WIDTH)),
                  _const_spec((GLA_WIDTH, GLA_WIDTH))],
        out_specs=[row(GLA_WIDTH), pl.BlockSpec((None, GLA_WIDTH, GLA_KEY_WIDTH), lambda b, i: (b, 0, 0))],
        out_shape=[jax.ShapeDtypeStruct((t, GLA_WIDTH), BF16),
                   jax.ShapeDtypeStruct((n, GLA_WIDTH, GLA_KEY_WIDTH), F32)],
        scratch_shapes=[pltpu.VMEM((GLA_WIDTH, GLA_KEY_WIDTH), F32), pltpu.VMEM((tc, GLA_WIDTH), F32)],
        compiler_params=_params("parallel", "arbitrary"),
        name="gla_prompt",
    )(qt, kt, kd, gv, eb, sg, gn_row, hsum)


def _merge_kernel(x_ref, sb_ref, gla_ref, cm_ref, w_ref, o_ref):
    o = _dot(sb_ref[...], w_ref[0:SB_WIDTH, :])
    o = o + _dot(gla_ref[...], w_ref[SB_WIDTH:SB_WIDTH + GLA_WIDTH, :])
    o = o + _dot(cm_ref[...], w_ref[SB_WIDTH + GLA_WIDTH:, :])
    o_ref[...] = x_ref[...] + o


def _merge(x, o_sb, o_gla, o_cm, w_out, *, tm):
    t, d = x.shape
    row = lambda w: pl.BlockSpec((tm, w), lambda i: (i, 0))
    return pl.pallas_call(
        _merge_kernel,
        grid=(t // tm,),
        in_specs=[row(d), row(SB_WIDTH), row(GLA_WIDTH), row(CM_WIDTH), _const_spec((d, d))],
        out_specs=row(d),
        out_shape=jax.ShapeDtypeStruct((t, d), F32),
        compiler_params=_params("parallel"),
        name="merge_heads",
    )(x, o_sb, o_gla, o_cm, w_out)


def _sb_sample_kernel(pt_ref, q_ref, kown_ref, vown_ref, bias_ref, u_ref, *refs, pages, past):
    k_refs, v_refs = refs[:pages], refs[pages:2 * pages]
    o_ref, qb_ref, acc_ref, carry_ref = refs[2 * pages:]
    s = pl.program_id(1)
    bias = bias_ref[...]

    @pl.when(s == 0)
    def _():
        q = q_ref[...]
        qb_ref[...] = jnp.broadcast_to(q, qb_ref.shape)
        key_pos = past + 0 * lax.broadcasted_iota(jnp.int32, (SB_HEADS, 1), 0)
        valid = key_pos < past
        z = jnp.sum(q * kown_ref[...], axis=1) + bias
        sp = jnp.where(valid, _softplus(z), 0.0)
        a = jnp.where(valid, jnp.exp(z - sp), 0.0)
        lane = lax.broadcasted_iota(jnp.int32, acc_ref.shape, 2)
        own = jnp.broadcast_to(a[:, :, None] * vown_ref[...], acc_ref.shape)
        acc_ref[...] = jnp.where(lane == 0, own, 0.0)
        carry_ref[...] = -sp

    carry = carry_ref[...]
    qb = qb_ref[...]
    for p in reversed(range(pages)):
        z = jnp.sum(k_refs[p][...] * qb, axis=1) + bias
        sp = _softplus(z)
        hi = sp.astype(BF16).astype(F32)
        lo = (sp - hi).astype(BF16).astype(F32)
        later = _dot(hi, u_ref[...]) + _dot(lo, u_ref[...])
        a = jnp.exp(z - sp - later + carry)
        acc_ref[...] += a[:, None, :] * v_refs[p][...]
        carry = carry - jnp.sum(sp, axis=1, keepdims=True)
    carry_ref[...] = carry

    @pl.when(s == pl.num_programs(1) - 1)
    def _():
        o_ref[...] = jnp.sum(acc_ref[...], axis=2, keepdims=True)


def _sb_sample(page_table, q, k_own, v_own, bias_col, u, cache_kt, cache_vt, *, layer, pages):
    nb, n_pages = page_table.shape
    n_steps = n_pages // pages
    past = n_pages * PAGE_SIZE
    page_blk = (None, None, SB_HEADS, HEAD_DIM, PAGE_SIZE)

    def page_spec(p):
        return pl.BlockSpec(page_blk, lambda b, s, pt: (layer, pt[b, (n_steps - 1 - s) * pages + p], 0, 0, 0))

    per_b = pl.BlockSpec((None, SB_HEADS, HEAD_DIM, 1), lambda b, s, pt: (b, 0, 0, 0))
    grid_spec = pltpu.PrefetchScalarGridSpec(
        num_scalar_prefetch=1,
        grid=(nb, n_steps),
        in_specs=[per_b, per_b, per_b,
                  pl.BlockSpec((SB_HEADS, 1), lambda b, s, pt: (0, 0)),
                  pl.BlockSpec((PAGE_SIZE, PAGE_SIZE), lambda b, s, pt: (0, 0))]
                 + [page_spec(p) for p in range(pages)] * 2,
        out_specs=per_b,
        scratch_shapes=[pltpu.VMEM((SB_HEADS, HEAD_DIM, PAGE_SIZE), F32),
                        pltpu.VMEM((SB_HEADS, HEAD_DIM, PAGE_SIZE), F32),
                        pltpu.VMEM((SB_HEADS, 1), F32)],
    )
    return pl.pallas_call(
        functools.partial(_sb_sample_kernel, pages=pages, past=past),
        grid_spec=grid_spec,
        out_shape=jax.ShapeDtypeStruct((nb, SB_HEADS, HEAD_DIM, 1), F32),
        compiler_params=_params("parallel", "arbitrary"),
        name="sb_sample",
    )(page_table, q, k_own, v_own, bias_col, u, *([cache_kt] * pages), *([cache_vt] * pages))


def _gla_sample_kernel(a_ref, s_ref, k_ref, v_ref, q_ref, sg_ref, gn_ref, s_out, og_out):
    s_new = a_ref[...] * s_ref[...] + k_ref[...] * v_ref[...]
    s_out[...] = s_new
    o = jnp.sum(q_ref[...] * s_new, axis=2)
    og_out[...] = _rms(o, gn_ref[...]) * sg_ref[...]


def _gla_sample(a, s0, gk, gv, gq, sg, gn):
    nb = s0.shape[0]
    col = lambda t: t.reshape(nb, GLA_HEADS, GLA_DK, 1)
    args = (col(a), s0, col(gk), gv.reshape(nb, GLA_HEADS, 1, GLA_DV), col(gq),
            sg.reshape(nb, GLA_HEADS, GLA_DV), gn.reshape(1, 1, GLA_DV))
    return pl.pallas_call(
        _gla_sample_kernel,
        out_shape=[jax.ShapeDtypeStruct(s0.shape, F32), jax.ShapeDtypeStruct((nb, GLA_HEADS, GLA_DV), F32)],
        name="gla_sample",
    )(*args)


def _tile(t, target):
    return target if t % target == 0 else t


def kernel(x_prompt, x_sample, cache_k, cache_v, state_gla, page_table, ffn1_norm_g, ffn1_w_up, ffn1_w_down, mix_norm_g, w_in, q_norm_g, k_norm_g, sb_logit_bias, gla_w_gate2, gla_b_gate, gla_out_norm_g, cm_ln_g, cm_ln_b, cm_w_spatial, cm_b_spatial, w_out, ffn2_norm_g, ffn2_w_up, ffn2_w_down):
    nb_p, L, d = x_prompt.shape
    nb_s, n_dec, _ = x_sample.shape
    depth = w_in.shape[0]
    assert n_dec == 1 and d == D_MODEL and L % CM_CHUNK == 0
    tp = nb_p * L
    tm = _tile(L, 512)
    tq = _tile(L, 256)
    tf = D_FF // 2
    pages = 8 if page_table.shape[1] % 8 == 0 else 1

    idx = jnp.arange(tm)
    same_chunk = (idx[:, None] // GLA_CHUNK) == (idx[None, :] // GLA_CHUNK)
    cum = (same_chunk & (idx[None, :] <= idx[:, None])).astype(BF16)
    tot = same_chunk.astype(BF16)
    hsum512 = ((jnp.arange(SB_WIDTH)[:, None] // HEAD_DIM) == (jnp.arange(SB_WIDTH)[None, :] // HEAD_DIM)).astype(BF16)
    hsum256 = hsum512[:GLA_WIDTH, :GLA_WIDTH]
    iq = jnp.arange(tq)
    u_later = jnp.tile((iq[:, None] > iq[None, :]).astype(BF16), (2, 1))
    ip = jnp.arange(PAGE_SIZE)
    u_page = (ip[:, None] > ip[None, :]).astype(F32)
    cache_kt = jnp.transpose(cache_k, (0, 1, 3, 4, 2))
    cache_vt = jnp.transpose(cache_v, (0, 1, 3, 4, 2))

    xp = x_prompt.reshape(tp, d)
    xs = x_sample.reshape(nb_s * n_dec, d)
    row = lambda v: v.reshape(1, -1).astype(F32)
    outs = {k: [] for k in ("kp", "vp", "sp", "ks", "vs", "ss", "cv")}
    for l in range(depth):
        w = w_in[l]
        w_r = jnp.concatenate([w[:, :C_GG], w[:, C_GG + GLA_RANK:], w[:, C_GG:C_GG + GLA_RANK],
                               jnp.zeros((d, N_IN_PAD - w.shape[1]), w.dtype)], axis=1).astype(BF16)
        wg2 = jnp.concatenate([gla_w_gate2[l], jnp.zeros((LANES - GLA_RANK, GLA_KEY_WIDTH), F32)], axis=0)
        mw = (row(mix_norm_g[l]), w_r, row(jnp.tile(q_norm_g[l], SB_HEADS)), row(jnp.tile(k_norm_g[l], SB_HEADS)),
              wg2, row(gla_b_gate[l]), row(cm_ln_g[l]), row(cm_ln_b[l]), hsum512)
        f1 = (row(ffn1_norm_g[l]), ffn1_w_up[l].astype(BF16), ffn1_w_down[l].astype(BF16))
        f2 = (row(ffn2_norm_g[l]), ffn2_w_up[l].astype(BF16), ffn2_w_down[l].astype(BF16))
        wo = w_out[l].astype(BF16)
        gn_row = row(jnp.tile(gla_out_norm_g[l], GLA_HEADS))
        bs_rows = jnp.repeat(cm_b_spatial[l].T, CM_GROUP_DIM, axis=1)

        xp = _ffn(xp, *f1, tm=tm, tf=tf)
        (q, kf, vf, kb, vb, qt, kt, kd, gv, eb, sg, ocm) = _mix_prompt(xp, mw, cum, tot, cm_w_spatial[l], bs_rows, tm=tm)
        r3 = lambda t: t.reshape(nb_p, L, SB_WIDTH)
        o_sb = _sb_prompt(r3(q), r3(kb), r3(vb), sb_logit_bias[l], u_later, tq=tq).reshape(tp, SB_WIDTH)
        o_gla, st = _gla_prompt(qt, kt, kd, gv, eb, sg, gn_row, hsum256, n=nb_p, L=L, tc=tm)
        xp = _merge(xp, o_sb, o_gla, ocm, wo, tm=tm)
        xp = _ffn(xp, *f2, tm=tm, tf=tf)
        outs["kp"].append(kf.reshape(nb_p, L, SB_HEADS, HEAD_DIM))
        outs["vp"].append(vf.reshape(nb_p, L, SB_HEADS, HEAD_DIM))
        st5 = st.reshape(nb_p, GLA_HEADS, GLA_DV, GLA_HEADS, GLA_DK)
        outs["sp"].append(jnp.stack([st5[:, h, :, h, :] for h in range(GLA_HEADS)], axis=1).transpose(0, 1, 3, 2))

        ts = xs.shape[0]
        xs = _ffn(xs, *f1, tm=ts, tf=tf)
        w00_row = row(jnp.repeat(cm_w_spatial[l][:, 0, 0], CM_GROUP_DIM))
        b0_row = row(jnp.repeat(cm_b_spatial[l][:, 0], CM_GROUP_DIM))
        (q, k, v, gq, gk, gv, a, sg, cv, ocm) = _mix_sample(xs, mw, w00_row, b0_row)
        col4 = lambda t: t.reshape(nb_s, SB_HEADS, HEAD_DIM, 1)
        o_sb = _sb_sample(page_table, col4(q), col4(k), col4(v), sb_logit_bias[l].reshape(SB_HEADS, 1), u_page,
                          cache_kt, cache_vt, layer=l, pages=pages).reshape(ts, SB_WIDTH)
        s_new, og = _gla_sample(a, state_gla[l], gk, gv, gq, sg, gla_out_norm_g[l])
        xs = _merge(xs, o_sb.astype(BF16), og.reshape(ts, GLA_WIDTH).astype(BF16), ocm, wo, tm=ts)
        xs = _ffn(xs, *f2, tm=ts, tf=tf)
        outs["ks"].append(k.reshape(nb_s, n_dec, SB_HEADS, HEAD_DIM))
        outs["vs"].append(v.reshape(nb_s, n_dec, SB_HEADS, HEAD_DIM))
        outs["ss"].append(s_new)
        outs["cv"].append(cv.reshape(nb_s, n_dec, CM_GROUPS, CM_GROUP_DIM))

    st = lambda k: jnp.stack(outs[k])
    return (xp.reshape(nb_p, L, d), xs.reshape(nb_s, n_dec, d), st("kp"), st("vp"), st("sp"),
            st("ks"), st("vs"), st("ss"), st("cv"))
```

```python
import functools

import jax
import jax.numpy as jnp
from jax import lax
from jax.experimental import pallas as pl
from jax.experimental.pallas import tpu as pltpu

F32 = jnp.float32
BF16 = jnp.bfloat16

D_MODEL = 1024
HEAD_DIM = 64
SB_HEADS = 8
SB_WIDTH = SB_HEADS * HEAD_DIM
GLA_HEADS = 4
GLA_DV = 64
GLA_DK = 32
GLA_WIDTH = GLA_HEADS * GLA_DV
GLA_KEY_WIDTH = GLA_HEADS * GLA_DK
GLA_RANK = 16
GLA_TAU = 16.0
GLA_CHUNK = 64
CM_GROUPS = 4
CM_WIDTH = 256
CM_GROUP_DIM = 64
CM_CHUNK = 128
D_FF = 2816
EPS = 1e-6
PAGE_SIZE = 128

LOG2E = 1.4426950408889634
MASKED_LOG = -1e30
UNROLL = 4
LANES = 128
VMEM_LIMIT = 56 * 1024 * 1024

C_SQ, C_SK, C_SV = 0, 512, 1024
C_GQ, C_GK, C_GV, C_GG = 1536, 1664, 1792, 2048
C_CU, C_CV, C_LR = 2304, 2560, 2816
N_IN_PAD = 2944

NT = (((1,), (1,)), ((), ()))
TN = (((0,), (0,)), ((), ()))


def _params(*sem):
    return pltpu.CompilerParams(dimension_semantics=sem, vmem_limit_bytes=VMEM_LIMIT)


def _const_spec(shape):
    n = len(shape)
    return pl.BlockSpec(shape, lambda *_: (0,) * n)


def _softplus(z):
    return jnp.maximum(z, 0.0) + jnp.log1p(jnp.exp(-jnp.abs(z)))


def _softplus2(x):
    neg_abs = pltpu.bitcast(pltpu.bitcast(x, jnp.uint32) | jnp.uint32(0x80000000), F32)
    return jnp.maximum(x, 0.0) + jnp.log2(1.0 + jnp.exp2(neg_abs))


def _split_bf16(x):
    hi = x.astype(BF16)
    lo = (x - hi.astype(F32)).astype(BF16)
    return hi, lo


def _dot(a, b):
    return jnp.dot(a, b, preferred_element_type=F32)


def _rms(x, g_row):
    return x * lax.rsqrt(jnp.mean(x * x, axis=-1, keepdims=True) + EPS) * g_row


def _ffn_kernel(x_ref, g_ref, wg_ref, wu_ref, wd_ref, o_ref, h_ref, acc_ref, *, nf):
    f = pl.program_id(1)

    @pl.when(f == 0)
    def _():
        h_ref[...] = _rms(x_ref[...], g_ref[...]).astype(BF16)
        acc_ref[...] = jnp.zeros_like(acc_ref)

    h = h_ref[...]
    gate = _dot(h, wg_ref[...])
    up = _dot(h, wu_ref[...])
    act = (gate * jax.nn.sigmoid(gate) * up).astype(BF16)
    acc_ref[...] += _dot(act, wd_ref[...])

    @pl.when(f == nf - 1)
    def _():
        o_ref[...] = x_ref[...] + 0.5 * acc_ref[...]


def _ffn(x, g_row, w_up, w_down, *, tm, tf):
    t, d = x.shape
    nf = D_FF // tf
    return pl.pallas_call(
        functools.partial(_ffn_kernel, nf=nf),
        grid=(t // tm, nf),
        in_specs=[
            pl.BlockSpec((tm, d), lambda i, f: (i, 0)),
            _const_spec((1, d)),
            pl.BlockSpec((d, tf), lambda i, f: (0, f)),
            pl.BlockSpec((d, tf), lambda i, f: (0, f + nf)),
            pl.BlockSpec((tf, d), lambda i, f: (f, 0)),
        ],
        out_specs=pl.BlockSpec((tm, d), lambda i, f: (i, 0)),
        out_shape=jax.ShapeDtypeStruct((t, d), F32),
        scratch_shapes=[pltpu.VMEM((tm, d), BF16), pltpu.VMEM((tm, d), F32)],
        compiler_params=_params("parallel", "arbitrary"),
        name="half_ffn",
    )(x, g_row, w_up, w_up, w_down)


def _mix_common(x_ref, g_ref, w_ref, qg_ref, kg_ref, wg2_ref, bg_ref, lng_ref, lnb_ref, hsum_ref):
    h = _rms(x_ref[...], g_ref[...]).astype(BF16)

    def proj(lo, width):
        return _dot(h, w_ref[:, lo:lo + width])

    def head_norm(t, gain_row):
        ms = _dot((t * t).astype(BF16), hsum_ref[...]) * (1.0 / HEAD_DIM)
        return t * lax.rsqrt(ms + EPS) * gain_row

    out = {}
    out["q"] = head_norm(proj(C_SQ, SB_WIDTH), qg_ref[...]) * (HEAD_DIM ** -0.5)
    out["k"] = head_norm(proj(C_SK, SB_WIDTH), kg_ref[...])
    out["v"] = proj(C_SV, SB_WIDTH)
    out["gq"] = proj(C_GQ, GLA_KEY_WIDTH) * (GLA_DK ** -0.5)
    out["gk"] = proj(C_GK, GLA_KEY_WIDTH)
    out["gv"] = proj(C_GV, GLA_WIDTH)
    gg = proj(C_GG, GLA_WIDTH)
    out["sg"] = gg * jax.nn.sigmoid(gg)
    lr_hi, lr_lo = _split_bf16(proj(C_LR, LANES))
    w_hi, w_lo = _split_bf16(wg2_ref[...])
    pre = _dot(lr_hi, w_hi) + _dot(lr_lo, w_hi) + _dot(lr_hi, w_lo) + bg_ref[...]
    out["la"] = -_softplus(-pre) * (1.0 / GLA_TAU)
    out["cu"] = jax.nn.gelu(proj(C_CU, CM_WIDTH))
    c = jax.nn.gelu(proj(C_CV, CM_WIDTH))
    c = c - jnp.mean(c, axis=-1, keepdims=True)
    out["cv"] = c * lax.rsqrt(jnp.mean(c * c, axis=-1, keepdims=True) + EPS) * lng_ref[...] + lnb_ref[...]
    return out


def _mix_prompt_kernel(x_ref, g_ref, w_ref, qg_ref, kg_ref, wg2_ref, bg_ref, lng_ref, lnb_ref, hsum_ref,
                       cum_ref, tot_ref, ws_ref, bs_ref,
                       q_o, kf_o, vf_o, kb_o, vb_o, qt_o, kt_o, kd_o, gv_o, eb_o, sg_o, ocm_o, *, tm):
    m = _mix_common(x_ref, g_ref, w_ref, qg_ref, kg_ref, wg2_ref, bg_ref, lng_ref, lnb_ref, hsum_ref)
    q_o[...] = (m["q"] * LOG2E).astype(BF16)
    kf_o[...] = m["k"].T
    kb_o[...] = m["k"].astype(BF16)
    vf_o[...] = m["v"].T
    vb_o[...] = m["v"].astype(BF16)
    gv_o[...] = m["gv"].astype(BF16)
    sg_o[...] = m["sg"]
    la_hi, la_lo = _split_bf16(m["la"])
    b = _dot(cum_ref[...], la_hi) + _dot(cum_ref[...], la_lo)
    b_last = _dot(tot_ref[...], la_hi) + _dot(tot_ref[...], la_lo)
    eb = jnp.exp(b)
    eb_o[...] = eb
    qt_o[...] = (m["gq"] * eb).astype(BF16)
    kt_o[...] = (m["gk"] * jnp.exp(-b)).astype(BF16)
    kd_o[...] = (m["gk"] * jnp.exp(b_last - b)).astype(BF16)
    row = lax.broadcasted_iota(jnp.int32, (CM_CHUNK, CM_CHUNK), 0)
    col = lax.broadcasted_iota(jnp.int32, (CM_CHUNK, CM_CHUNK), 1)
    group = lax.broadcasted_iota(jnp.int32, (1, CM_WIDTH), 1) // CM_GROUP_DIM
    w_tril = [jnp.where(row >= col, ws_ref[g], 0.0).astype(BF16) for g in range(CM_GROUPS)]
    cv = m["cv"].astype(BF16)
    for c in range(tm // CM_CHUNK):
        rows = slice(c * CM_CHUNK, (c + 1) * CM_CHUNK)
        mixed = bs_ref[...]
        for g in range(CM_GROUPS):
            mixed = mixed + _dot(w_tril[g], jnp.where(group == g, cv[rows], jnp.zeros_like(cv[rows])))
        ocm_o[rows, :] = (m["cu"][rows] * mixed).astype(BF16)


def _mix_sample_kernel(x_ref, g_ref, w_ref, qg_ref, kg_ref, wg2_ref, bg_ref, lng_ref, lnb_ref, hsum_ref,
                       w00_ref, b0_ref,
                       q_o, k_o, v_o, gq_o, gk_o, gv_o, a_o, sg_o, cv_o, ocm_o):
    m = _mix_common(x_ref, g_ref, w_ref, qg_ref, kg_ref, wg2_ref, bg_ref, lng_ref, lnb_ref, hsum_ref)
    q_o[...] = m["q"]
    k_o[...] = m["k"]
    v_o[...] = m["v"]
    gq_o[...] = m["gq"]
    gk_o[...] = m["gk"]
    gv_o[...] = m["gv"]
    a_o[...] = jnp.exp(m["la"])
    sg_o[...] = m["sg"]
    cv_o[...] = m["cv"]
    ocm_o[...] = (m["cu"] * (w00_ref[...] * m["cv"] + b0_ref[...])).astype(BF16)


def _mix_weight_specs():
    return [
        _const_spec((1, D_MODEL)),
        _const_spec((D_MODEL, N_IN_PAD)),
        _const_spec((1, SB_WIDTH)),
        _const_spec((1, SB_WIDTH)),
        _const_spec((LANES, GLA_KEY_WIDTH)),
        _const_spec((1, GLA_KEY_WIDTH)),
        _const_spec((1, CM_WIDTH)),
        _const_spec((1, CM_WIDTH)),
        _const_spec((SB_WIDTH, SB_WIDTH)),
    ]


def _mix_prompt(x, mw, cum, tot, w_s, bs_rows, *, tm, n, L):
    t = x.shape[0]
    nt = L // tm
    kv_spec = pl.BlockSpec((None, SB_WIDTH, tm), lambda i: (i // nt, 0, i % nt))
    kv_sds = jax.ShapeDtypeStruct((n, SB_WIDTH, L), F32)
    row = lambda w: pl.BlockSpec((tm, w), lambda i: (i, 0))
    sds = lambda w, dt: jax.ShapeDtypeStruct((t, w), dt)
    widths = [(SB_WIDTH, BF16), (SB_WIDTH, F32), (SB_WIDTH, F32), (SB_WIDTH, BF16), (SB_WIDTH, BF16),
              (GLA_KEY_WIDTH, BF16), (GLA_KEY_WIDTH, BF16), (GLA_KEY_WIDTH, BF16), (GLA_WIDTH, BF16),
              (GLA_KEY_WIDTH, F32), (GLA_WIDTH, F32), (CM_WIDTH, BF16)]
    return pl.pallas_call(
        functools.partial(_mix_prompt_kernel, tm=tm),
        grid=(t // tm,),
        in_specs=[row(D_MODEL)] + _mix_weight_specs() + [
            _const_spec((tm, tm)), _const_spec((tm, tm)),
            _const_spec((CM_GROUPS, CM_CHUNK, CM_CHUNK)), _const_spec((CM_CHUNK, CM_WIDTH))],
        out_specs=[kv_spec if i in (1, 2) else row(w) for i, (w, _) in enumerate(widths)],
        out_shape=[kv_sds if i in (1, 2) else sds(w, dt) for i, (w, dt) in enumerate(widths)],
        compiler_params=_params("parallel"),
        name="mix_prompt",
    )(x, *mw, cum, tot, w_s, bs_rows)


def _mix_sample(x, mw, w00_row, b0_row):
    t = x.shape[0]
    full = lambda w: _const_spec((t, w))
    widths = [(SB_WIDTH, F32), (SB_WIDTH, F32), (SB_WIDTH, F32), (GLA_KEY_WIDTH, F32), (GLA_KEY_WIDTH, F32),
              (GLA_WIDTH, F32), (GLA_KEY_WIDTH, F32), (GLA_WIDTH, F32), (CM_WIDTH, F32), (CM_WIDTH, BF16)]
    return pl.pallas_call(
        _mix_sample_kernel,
        grid=(1,),
        in_specs=[full(D_MODEL)] + _mix_weight_specs() + [_const_spec((1, CM_WIDTH)), _const_spec((1, CM_WIDTH))],
        out_specs=[full(w) for w, _ in widths],
        out_shape=[jax.ShapeDtypeStruct((t, w), dt) for w, dt in widths],
        compiler_params=_params("arbitrary"),
        name="mix_sample",
    )(x, *mw, w00_row, b0_row)


def _sb_prompt_kernel(bias_ref, q_ref, k_ref, v_ref, u_ref, o_ref, acc_ref, za_ref, zb_ref, w_ref, c_ref, r_ref,
                      *, tq):
    hp = pl.program_id(1)
    qi = pl.program_id(2)
    lane_head = lax.broadcasted_iota(jnp.int32, (1, LANES), 1) // HEAD_DIM
    q = q_ref[...]
    q2 = jnp.concatenate([jnp.where(lane_head == hh, q, jnp.zeros_like(q)) for hh in range(2)], axis=0)
    biases = [bias_ref[2 * hp + hh] * LOG2E for hh in range(2)]
    row = lax.broadcasted_iota(jnp.int32, (tq, tq), 0)
    col = lax.broadcasted_iota(jnp.int32, (tq, tq), 1)
    heads = [slice(hh * tq, (hh + 1) * tq) for hh in range(2)]

    def scores(j):
        kj = k_ref[pl.ds(pl.multiple_of(j * tq, tq), tq), :]
        return lax.dot_general(q2, kj, NT, preferred_element_type=F32)

    def log_terms(z_ref, diagonal):
        for hh in range(2):
            z = z_ref[heads[hh], :] + biases[hh]
            sp = _softplus2(z)
            if diagonal:
                sp = jnp.where(col < row, sp, 0.0)
            w = z - sp - _dot(sp.astype(BF16), u_ref[...])
            if diagonal:
                w = jnp.where(col < row, w, MASKED_LOG)
            w_ref[heads[hh], :] = w
            r_ref[heads[hh], :] = jnp.broadcast_to(jnp.sum(sp, axis=1, keepdims=True), (tq, LANES))

    def attend(j):
        vj = v_ref[pl.ds(pl.multiple_of(j * tq, tq), tq), :]
        a = []
        for hh in range(2):
            c = c_ref[heads[hh], :]
            a.append(jnp.exp2(w_ref[heads[hh], :] + jnp.concatenate([c] * (tq // LANES), axis=1)).astype(BF16))
        o2 = _dot(jnp.concatenate(a, axis=0), vj)
        acc_ref[...] += jnp.where(lane_head == 0, o2[:tq], o2[tq:])

    def step(n, z_cur, z_nxt):
        attend(qi - n + 1)
        c_ref[...] = c_ref[...] - r_ref[...]
        z_nxt[...] = scores(jnp.maximum(qi - n - 1, 0))
        log_terms(z_cur, False)

    acc_ref[...] = jnp.zeros_like(acc_ref)
    c_ref[...] = jnp.zeros_like(c_ref)
    za_ref[...] = scores(qi)
    log_terms(za_ref, True)
    zb_ref[...] = scores(jnp.maximum(qi - 1, 0))

    def body(i, _):
        for k in range(1, UNROLL + 1, 2):
            step(UNROLL * i + k, zb_ref, za_ref)
            step(UNROLL * i + k + 1, za_ref, zb_ref)
        return 0

    lax.fori_loop(0, qi // UNROLL, body, 0)
    done = (qi // UNROLL) * UNROLL

    @pl.when(qi - done >= 2)
    def _():
        step(done + 1, zb_ref, za_ref)
        step(done + 2, za_ref, zb_ref)

    @pl.when((qi - done) % 2 == 1)
    def _():
        step(qi, zb_ref, za_ref)

    attend(0)
    o_ref[...] = acc_ref[...].astype(BF16)


def _sb_prompt(q, k, v, bias, u, *, tq):
    n, L, _ = q.shape
    blk = lambda rows: (None, rows, LANES)
    return pl.pallas_call(
        functools.partial(_sb_prompt_kernel, tq=tq),
        grid=(n, SB_WIDTH // LANES, L // tq),
        in_specs=[
            pl.BlockSpec(memory_space=pltpu.SMEM),
            pl.BlockSpec(blk(tq), lambda b, h, i: (b, i, h)),
            pl.BlockSpec(blk(L), lambda b, h, i: (b, 0, h)),
            pl.BlockSpec(blk(L), lambda b, h, i: (b, 0, h)),
            _const_spec((tq, tq)),
        ],
        out_specs=pl.BlockSpec(blk(tq), lambda b, h, i: (b, i, h)),
        out_shape=jax.ShapeDtypeStruct((n, L, SB_WIDTH), BF16),
        scratch_shapes=[pltpu.VMEM((tq, LANES), F32)] + [pltpu.VMEM((2 * tq, tq), F32)] * 3
                       + [pltpu.VMEM((2 * tq, LANES), F32)] * 2,
        compiler_params=_params("parallel", "parallel", "arbitrary"),
        name="sb_prompt",
    )(bias, q, k, v, u)


def _gla_prompt_kernel(qt_ref, kt_ref, kd_ref, v_ref, eb_ref, sg_ref, gn_ref, hsum_ref,
                       og_ref, st_ref, s_ref, o_ref, *, tc):
    @pl.when(pl.program_id(1) == 0)
    def _():
        s_ref[...] = jnp.zeros_like(s_ref)

    ck = GLA_CHUNK
    key_head = lax.broadcasted_iota(jnp.int32, (1, GLA_KEY_WIDTH), 1) // GLA_DK
    val_head = lax.broadcasted_iota(jnp.int32, (1, GLA_WIDTH), 1) // GLA_DV
    state_mask = (lax.broadcasted_iota(jnp.int32, (GLA_WIDTH, GLA_KEY_WIDTH), 0) // GLA_DV
                  == lax.broadcasted_iota(jnp.int32, (GLA_WIDTH, GLA_KEY_WIDTH), 1) // GLA_DK)
    causal = (lax.broadcasted_iota(jnp.int32, (ck, ck), 0) >= lax.broadcasted_iota(jnp.int32, (ck, ck), 1))
    for c in range(tc // ck):
        rows = slice(c * ck, (c + 1) * ck)
        qt, kt, kd, v = qt_ref[rows, :], kt_ref[rows, :], kd_ref[rows, :], v_ref[rows, :]
        att = []
        v_bd = []
        for hh in range(GLA_HEADS):
            qh = jnp.where(key_head == hh, qt, jnp.zeros_like(qt))
            s = lax.dot_general(qh, kt, NT, preferred_element_type=F32)
            att.append(jnp.where(causal, s, 0.0).astype(BF16))
            v_bd.append(jnp.where(val_head == hh, v, jnp.zeros_like(v)))
        st = s_ref[...]
        o = _dot(jnp.concatenate(att, axis=1), jnp.concatenate(v_bd, axis=0))
        o = o + lax.dot_general(qt, st.astype(BF16), NT, preferred_element_type=F32)
        o_ref[rows, :] = o
        ds = lax.dot_general(v, kd, TN, preferred_element_type=F32)
        decay = eb_ref[c * ck + ck - 1:c * ck + ck, :]
        s_ref[...] = decay * st + jnp.where(state_mask, ds, 0.0)
    o = o_ref[...]
    ms = _dot((o * o).astype(BF16), hsum_ref[...]) * (1.0 / GLA_DV)
    og_ref[...] = (o * lax.rsqrt(ms + EPS) * gn_ref[...] * sg_ref[...]).astype(BF16)
    st_ref[...] = s_ref[...]


def _gla_prompt(qt, kt, kd, gv, eb, sg, gn_row, hsum, *, n, L, tc):
    t = n * L
    nc = L // tc
    row = lambda w: pl.BlockSpec((tc, w), lambda b, i: (b * nc + i, 0))
    return pl.pallas_call(
        functools.partial(_gla_prompt_kernel, tc=tc),
        grid=(n, nc),
        in_specs=[row(GLA_KEY_WIDTH), row(GLA_KEY_WIDTH), row(GLA_KEY_WIDTH), row(GLA_WIDTH),
                  row(GLA_KEY_WIDTH), row(GLA_WIDTH), _const_spec((1, GLA_WIDTH)),
                  _const_spec((GLA_WIDTH, GLA_WIDTH))],
        out_specs=[row(GLA_WIDTH), pl.BlockSpec((None, GLA_WIDTH, GLA_KEY_WIDTH), lambda b, i: (b, 0, 0))],
        out_shape=[jax.ShapeDtypeStruct((t, GLA_WIDTH), BF16),
                   jax.ShapeDtypeStruct((n, GLA_WIDTH, GLA_KEY_WIDTH), F32)],
        scratch_shapes=[pltpu.VMEM((GLA_WIDTH, GLA_KEY_WIDTH), F32), pltpu.VMEM((tc, GLA_WIDTH), F32)],
        compiler_params=_params("parallel", "arbitrary"),
        name="gla_prompt",
    )(qt, kt, kd, gv, eb, sg, gn_row, hsum)


def _merge_kernel(x_ref, sb_ref, gla_ref, cm_ref, w_ref, o_ref):
    o = _dot(sb_ref[...], w_ref[0:SB_WIDTH, :])
    o = o + _dot(gla_ref[...], w_ref[SB_WIDTH:SB_WIDTH + GLA_WIDTH, :])
    o = o + _dot(cm_ref[...], w_ref[SB_WIDTH + GLA_WIDTH:, :])
    o_ref[...] = x_ref[...] + o


def _merge(x, o_sb, o_gla, o_cm, w_out, *, tm):
    t, d = x.shape
    row = lambda w: pl.BlockSpec((tm, w), lambda i: (i, 0))
    return pl.pallas_call(
        _merge_kernel,
        grid=(t // tm,),
        in_specs=[row(d), row(SB_WIDTH), row(GLA_WIDTH), row(CM_WIDTH), _const_spec((d, d))],
        out_specs=row(d),
        out_shape=jax.ShapeDtypeStruct((t, d), F32),
        compiler_params=_params("parallel"),
        name="merge_heads",
    )(x, o_sb, o_gla, o_cm, w_out)


def _sb_sample_kernel(pt_ref, q_ref, kown_ref, vown_ref, bias_ref, u_ref, *refs, pages, past):
    k_refs, v_refs = refs[:pages], refs[pages:2 * pages]
    o_ref, qb_ref, acc_ref, carry_ref = refs[2 * pages:]
    s = pl.program_id(1)
    bias = bias_ref[...]

    @pl.when(s == 0)
    def _():
        q = q_ref[...]
        qb_ref[...] = jnp.broadcast_to(q, qb_ref.shape)
        key_pos = past + 0 * lax.broadcasted_iota(jnp.int32, (SB_HEADS, 1), 0)
        valid = key_pos < past
        z = jnp.sum(q * kown_ref[...], axis=1) + bias
        sp = jnp.where(valid, _softplus(z), 0.0)
        a = jnp.where(valid, jnp.exp(z - sp), 0.0)
        lane = lax.broadcasted_iota(jnp.int32, acc_ref.shape, 2)
        own = jnp.broadcast_to(a[:, :, None] * vown_ref[...], acc_ref.shape)
        acc_ref[...] = jnp.where(lane == 0, own, 0.0)
        carry_ref[...] = -sp

    order = list(reversed(range(pages)))
    qb = qb_ref[...]
    z = jnp.concatenate([jnp.sum(k_refs[p][...] * qb, axis=1) for p in order], axis=0)
    z = z + jnp.concatenate([bias] * pages, axis=0)
    sp = _softplus(z)
    hi = sp.astype(BF16).astype(F32)
    lo = (sp - hi).astype(BF16).astype(F32)
    later = _dot(hi, u_ref[...]) + _dot(lo, u_ref[...])
    totals = jnp.sum(sp, axis=1, keepdims=True)
    carries = [carry_ref[...]]
    for i in range(pages):
        carries.append(carries[-1] - totals[i * SB_HEADS:(i + 1) * SB_HEADS])
    carry_ref[...] = carries[-1]
    a = jnp.exp(z - sp - later + jnp.concatenate(carries[:-1], axis=0))
    for h in range(SB_HEADS):
        acc = acc_ref[h]
        for i, p in enumerate(order):
            acc = acc + a[i * SB_HEADS + h:i * SB_HEADS + h + 1, :] * v_refs[p][h]
        acc_ref[h] = acc

    @pl.when(s == pl.num_programs(1) - 1)
    def _():
        o_ref[...] = jnp.sum(acc_ref[...], axis=2, keepdims=True)


def _sb_sample(page_table, q, k_own, v_own, bias_col, u, cache_kt, cache_vt, *, layer, pages):
    nb, n_pages = page_table.shape
    n_steps = n_pages // pages
    past = n_pages * PAGE_SIZE
    page_blk = (None, None, SB_HEADS, HEAD_DIM, PAGE_SIZE)

    def page_spec(p):
        return pl.BlockSpec(page_blk, lambda b, s, pt: (layer, pt[b, (n_steps - 1 - s) * pages + p], 0, 0, 0))

    per_b = pl.BlockSpec((None, SB_HEADS, HEAD_DIM, 1), lambda b, s, pt: (b, 0, 0, 0))
    grid_spec = pltpu.PrefetchScalarGridSpec(
        num_scalar_prefetch=1,
        grid=(nb, n_steps),
        in_specs=[per_b, per_b, per_b,
                  pl.BlockSpec((SB_HEADS, 1), lambda b, s, pt: (0, 0)),
                  pl.BlockSpec((PAGE_SIZE, PAGE_SIZE), lambda b, s, pt: (0, 0))]
                 + [page_spec(p) for p in range(pages)] * 2,
        out_specs=per_b,
        scratch_shapes=[pltpu.VMEM((SB_HEADS, HEAD_DIM, PAGE_SIZE), F32),
                        pltpu.VMEM((SB_HEADS, HEAD_DIM, PAGE_SIZE), F32),
                        pltpu.VMEM((SB_HEADS, 1), F32)],
    )
    return pl.pallas_call(
        functools.partial(_sb_sample_kernel, pages=pages, past=past),
        grid_spec=grid_spec,
        out_shape=jax.ShapeDtypeStruct((nb, SB_HEADS, HEAD_DIM, 1), F32),
        compiler_params=_params("parallel", "arbitrary"),
        name="sb_sample",
    )(page_table, q, k_own, v_own, bias_col, u, *([cache_kt] * pages), *([cache_vt] * pages))


def _gla_sample_kernel(a_ref, s_ref, k_ref, v_ref, q_ref, sg_ref, gn_ref, s_out, og_out):
    s_new = a_ref[...] * s_ref[...] + k_ref[...] * v_ref[...]
    s_out[...] = s_new
    o = jnp.sum(q_ref[...] * s_new, axis=2)
    og_out[...] = _rms(o, gn_ref[...]) * sg_ref[...]


def _gla_sample(a, s0, gk, gv, gq, sg, gn):
    nb = s0.shape[0]
    col = lambda t: t.reshape(nb, GLA_HEADS, GLA_DK, 1)
    args = (col(a), s0, col(gk), gv.reshape(nb, GLA_HEADS, 1, GLA_DV), col(gq),
            sg.reshape(nb, GLA_HEADS, GLA_DV), gn.reshape(1, 1, GLA_DV))
    return pl.pallas_call(
        _gla_sample_kernel,
        out_shape=[jax.ShapeDtypeStruct(s0.shape, F32), jax.ShapeDtypeStruct((nb, GLA_HEADS, GLA_DV), F32)],
        name="gla_sample",
    )(*args)


def _tile(t, target):
    return target if t % target == 0 else t


def kernel(x_prompt, x_sample, cache_k, cache_v, state_gla, page_table, ffn1_norm_g, ffn1_w_up, ffn1_w_down, mix_norm_g, w_in, q_norm_g, k_norm_g, sb_logit_bias, gla_w_gate2, gla_b_gate, gla_out_norm_g, cm_ln_g, cm_ln_b, cm_w_spatial, cm_b_spatial, w_out, ffn2_norm_g, ffn2_w_up, ffn2_w_down):
    nb_p, L, d = x_prompt.shape
    nb_s, n_dec, _ = x_sample.shape
    depth = w_in.shape[0]
    assert n_dec == 1 and d == D_MODEL and L % CM_CHUNK == 0
    tp = nb_p * L
    tm = _tile(L, 512)
    tq = _tile(L, 256)
    tf = D_FF // 2
    pages = 8 if page_table.shape[1] % 8 == 0 else 1

    idx = jnp.arange(tm)
    same_chunk = (idx[:, None] // GLA_CHUNK) == (idx[None, :] // GLA_CHUNK)
    cum = (same_chunk & (idx[None, :] <= idx[:, None])).astype(BF16)
    tot = same_chunk.astype(BF16)
    hsum512 = ((jnp.arange(SB_WIDTH)[:, None] // HEAD_DIM) == (jnp.arange(SB_WIDTH)[None, :] // HEAD_DIM)).astype(BF16)
    hsum256 = hsum512[:GLA_WIDTH, :GLA_WIDTH]
    iq = jnp.arange(tq)
    u_later = (iq[:, None] > iq[None, :]).astype(BF16)
    ip = jnp.arange(PAGE_SIZE)
    u_page = (ip[:, None] > ip[None, :]).astype(F32)
    cache_kt = jnp.transpose(cache_k, (0, 1, 3, 4, 2))
    cache_vt = jnp.transpose(cache_v, (0, 1, 3, 4, 2))

    xp = x_prompt.reshape(tp, d)
    xs = x_sample.reshape(nb_s * n_dec, d)
    row = lambda v: v.reshape(1, -1).astype(F32)
    outs = {k: [] for k in ("kp", "vp", "sp", "ks", "vs", "ss", "cv")}
    for l in range(depth):
        w = w_in[l]
        w_r = jnp.concatenate([w[:, :C_GG], w[:, C_GG + GLA_RANK:], w[:, C_GG:C_GG + GLA_RANK],
                               jnp.zeros((d, N_IN_PAD - w.shape[1]), w.dtype)], axis=1).astype(BF16)
        wg2 = jnp.concatenate([gla_w_gate2[l], jnp.zeros((LANES - GLA_RANK, GLA_KEY_WIDTH), F32)], axis=0)
        mw = (row(mix_norm_g[l]), w_r, row(jnp.tile(q_norm_g[l], SB_HEADS)), row(jnp.tile(k_norm_g[l], SB_HEADS)),
              wg2, row(gla_b_gate[l]), row(cm_ln_g[l]), row(cm_ln_b[l]), hsum512)
        f1 = (row(ffn1_norm_g[l]), ffn1_w_up[l].astype(BF16), ffn1_w_down[l].astype(BF16))
        f2 = (row(ffn2_norm_g[l]), ffn2_w_up[l].astype(BF16), ffn2_w_down[l].astype(BF16))
        wo = w_out[l].astype(BF16)
        gn_row = row(jnp.tile(gla_out_norm_g[l], GLA_HEADS))
        bs_rows = jnp.repeat(cm_b_spatial[l].T, CM_GROUP_DIM, axis=1)

        xp = _ffn(xp, *f1, tm=tm, tf=tf)
        (q, kf, vf, kb, vb, qt, kt, kd, gv, eb, sg, ocm) = _mix_prompt(xp, mw, cum, tot, cm_w_spatial[l], bs_rows, tm=tm,
                                                                         n=nb_p, L=L)
        r3 = lambda t: t.reshape(nb_p, L, SB_WIDTH)
        o_sb = _sb_prompt(r3(q), r3(kb), r3(vb), sb_logit_bias[l], u_later, tq=tq).reshape(tp, SB_WIDTH)
        o_gla, st = _gla_prompt(qt, kt, kd, gv, eb, sg, gn_row, hsum256, n=nb_p, L=L, tc=tm)
        xp = _merge(xp, o_sb, o_gla, ocm, wo, tm=tm)
        xp = _ffn(xp, *f2, tm=tm, tf=tf)
        to_cache = lambda t: t.reshape(nb_p, SB_HEADS, HEAD_DIM, L).transpose(0, 3, 1, 2)
        outs["kp"].append(to_cache(kf))
        outs["vp"].append(to_cache(vf))
        st5 = st.reshape(nb_p, GLA_HEADS, GLA_DV, GLA_HEADS, GLA_DK)
        outs["sp"].append(jnp.stack([st5[:, h, :, h, :] for h in range(GLA_HEADS)], axis=1).transpose(0, 1, 3, 2))

        ts = xs.shape[0]
        xs = _ffn(xs, *f1, tm=ts, tf=tf)
        w00_row = row(jnp.repeat(cm_w_spatial[l][:, 0, 0], CM_GROUP_DIM))
        b0_row = row(jnp.repeat(cm_b_spatial[l][:, 0], CM_GROUP_DIM))
        (q, k, v, gq, gk, gv, a, sg, cv, ocm) = _mix_sample(xs, mw, w00_row, b0_row)
        col4 = lambda t: t.reshape(nb_s, SB_HEADS, HEAD_DIM, 1)
        o_sb = _sb_sample(page_table, col4(q), col4(k), col4(v), sb_logit_bias[l].reshape(SB_HEADS, 1), u_page,
                          cache_kt, cache_vt, layer=l, pages=pages).reshape(ts, SB_WIDTH)
        s_new, og = _gla_sample(a, state_gla[l], gk, gv, gq, sg, gla_out_norm_g[l])
        xs = _merge(xs, o_sb.astype(BF16), og.reshape(ts, GLA_WIDTH).astype(BF16), ocm, wo, tm=ts)
        xs = _ffn(xs, *f2, tm=ts, tf=tf)
        outs["ks"].append(k.reshape(nb_s, n_dec, SB_HEADS, HEAD_DIM))
        outs["vs"].append(v.reshape(nb_s, n_dec, SB_HEADS, HEAD_DIM))
        outs["ss"].append(s_new)
        outs["cv"].append(cv.reshape(nb_s, n_dec, CM_GROUPS, CM_GROUP_DIM))

    st = lambda k: jnp.stack(outs[k])
    return (xp.reshape(nb_p, L, d), xs.reshape(nb_s, n_dec, d), st("kp"), st("vp"), st("sp"),
            st("ks"), st("vs"), st("ss"), st("cv"))
```

```python
import functools

import jax
import jax.numpy as jnp
from jax import lax
from jax.experimental import pallas as pl
from jax.experimental.pallas import tpu as pltpu

F32 = jnp.float32
BF16 = jnp.bfloat16

D_MODEL = 1024
HEAD_DIM = 64
SB_HEADS = 8
SB_WIDTH = SB_HEADS * HEAD_DIM
GLA_HEADS = 4
GLA_DV = 64
GLA_DK = 32
GLA_WIDTH = GLA_HEADS * GLA_DV
GLA_KEY_WIDTH = GLA_HEADS * GLA_DK
GLA_RANK = 16
GLA_TAU = 16.0
GLA_CHUNK = 64
CM_GROUPS = 4
CM_WIDTH = 256
CM_GROUP_DIM = 64
CM_CHUNK = 128
D_FF = 2816
EPS = 1e-6
PAGE_SIZE = 128

LOG2E = 1.4426950408889634
MASKED_LOG = -1e30
UNROLL = 4
LANES = 128
VMEM_LIMIT = 56 * 1024 * 1024

C_SQ, C_SK, C_SV = 0, 512, 1024
C_GQ, C_GK, C_GV, C_GG = 1536, 1664, 1792, 2048
C_CU, C_CV, C_LR = 2304, 2560, 2816
N_IN_PAD = 2944

NT = (((1,), (1,)), ((), ()))
TN = (((0,), (0,)), ((), ()))


def _params(*sem):
    return pltpu.CompilerParams(dimension_semantics=sem, vmem_limit_bytes=VMEM_LIMIT)


def _const_spec(shape):
    n = len(shape)
    return pl.BlockSpec(shape, lambda *_: (0,) * n)


def _softplus(z):
    return jnp.maximum(z, 0.0) + jnp.log1p(jnp.exp(-jnp.abs(z)))


def _softplus2(x):
    neg_abs = pltpu.bitcast(pltpu.bitcast(x, jnp.uint32) | jnp.uint32(0x80000000), F32)
    return jnp.maximum(x, 0.0) + jnp.log2(1.0 + jnp.exp2(neg_abs))


def _split_bf16(x):
    hi = x.astype(BF16)
    lo = (x - hi.astype(F32)).astype(BF16)
    return hi, lo


def _dot(a, b):
    return jnp.dot(a, b, preferred_element_type=F32)


def _rms(x, g_row):
    return x * lax.rsqrt(jnp.mean(x * x, axis=-1, keepdims=True) + EPS) * g_row


def _ffn_kernel(*refs, nf, merged):
    if merged:
        x_ref, sb_ref, gla_ref, cm_ref, wo_ref, g_ref, wg_ref, wu_ref, wd_ref, o_ref, h_ref, acc_ref, x1_ref = refs
    else:
        x_ref, g_ref, wg_ref, wu_ref, wd_ref, o_ref, h_ref, acc_ref = refs
        x1_ref = x_ref
    f = pl.program_id(1)

    @pl.when(f == 0)
    def _():
        if merged:
            o = _dot(sb_ref[...], wo_ref[0:SB_WIDTH, :])
            o = o + _dot(gla_ref[...], wo_ref[SB_WIDTH:SB_WIDTH + GLA_WIDTH, :])
            o = o + _dot(cm_ref[...], wo_ref[SB_WIDTH + GLA_WIDTH:, :])
            x1_ref[...] = x_ref[...] + o
        h_ref[...] = _rms(x1_ref[...], g_ref[...]).astype(BF16)
        acc_ref[...] = jnp.zeros_like(acc_ref)

    h = h_ref[...]
    gate = _dot(h, wg_ref[...])
    up = _dot(h, wu_ref[...])
    act = (gate * jax.nn.sigmoid(gate) * up).astype(BF16)
    acc_ref[...] += _dot(act, wd_ref[...])

    @pl.when(f == nf - 1)
    def _():
        o_ref[...] = x1_ref[...] + 0.5 * acc_ref[...]


def _ffn(x, g_row, w_up, w_down, *, tm, tf, merge=None):
    t, d = x.shape
    nf = D_FF // tf
    row = lambda w: pl.BlockSpec((tm, w), lambda i, f: (i, 0))
    merge_specs, merge_args, scratch = [], (), []
    if merge is not None:
        merge_specs = [row(SB_WIDTH), row(GLA_WIDTH), row(CM_WIDTH), _const_spec((d, d))]
        merge_args = tuple(merge)
        scratch = [pltpu.VMEM((tm, d), F32)]
    return pl.pallas_call(
        functools.partial(_ffn_kernel, nf=nf, merged=merge is not None),
        grid=(t // tm, nf),
        in_specs=[row(d)] + merge_specs + [
            _const_spec((1, d)),
            pl.BlockSpec((d, tf), lambda i, f: (0, f)),
            pl.BlockSpec((d, tf), lambda i, f: (0, f + nf)),
            pl.BlockSpec((tf, d), lambda i, f: (f, 0)),
        ],
        out_specs=row(d),
        out_shape=jax.ShapeDtypeStruct((t, d), F32),
        scratch_shapes=[pltpu.VMEM((tm, d), BF16), pltpu.VMEM((tm, d), F32)] + scratch,
        compiler_params=_params("parallel", "arbitrary"),
        name="half_ffn_merged" if merge is not None else "half_ffn",
    )(x, *merge_args, g_row, w_up, w_up, w_down)


def _mix_common(x_ref, g_ref, w_ref, qg_ref, kg_ref, wg2_ref, bg_ref, lng_ref, lnb_ref, hsum_ref):
    h = _rms(x_ref[...], g_ref[...]).astype(BF16)

    def proj(lo, width):
        return _dot(h, w_ref[:, lo:lo + width])

    def head_norm(t, gain_row):
        ms = _dot((t * t).astype(BF16), hsum_ref[...]) * (1.0 / HEAD_DIM)
        return t * lax.rsqrt(ms + EPS) * gain_row

    out = {}
    out["q"] = head_norm(proj(C_SQ, SB_WIDTH), qg_ref[...]) * (HEAD_DIM ** -0.5)
    out["k"] = head_norm(proj(C_SK, SB_WIDTH), kg_ref[...])
    out["v"] = proj(C_SV, SB_WIDTH)
    out["gq"] = proj(C_GQ, GLA_KEY_WIDTH) * (GLA_DK ** -0.5)
    out["gk"] = proj(C_GK, GLA_KEY_WIDTH)
    out["gv"] = proj(C_GV, GLA_WIDTH)
    gg = proj(C_GG, GLA_WIDTH)
    out["sg"] = gg * jax.nn.sigmoid(gg)
    lr_hi, lr_lo = _split_bf16(proj(C_LR, LANES))
    w_hi, w_lo = _split_bf16(wg2_ref[...])
    pre = _dot(lr_hi, w_hi) + _dot(lr_lo, w_hi) + _dot(lr_hi, w_lo) + bg_ref[...]
    out["la"] = -_softplus(-pre) * (1.0 / GLA_TAU)
    out["cu"] = jax.nn.gelu(proj(C_CU, CM_WIDTH))
    c = jax.nn.gelu(proj(C_CV, CM_WIDTH))
    c = c - jnp.mean(c, axis=-1, keepdims=True)
    out["cv"] = c * lax.rsqrt(jnp.mean(c * c, axis=-1, keepdims=True) + EPS) * lng_ref[...] + lnb_ref[...]
    return out


def _mix_prompt_kernel(x_ref, g_ref, w_ref, qg_ref, kg_ref, wg2_ref, bg_ref, lng_ref, lnb_ref, hsum_ref,
                       cum_ref, tot_ref, ws_ref, bs_ref,
                       q_o, kf_o, vf_o, kb_o, vb_o, qt_o, kt_o, kd_o, gv_o, eb_o, sg_o, ocm_o, *, tm):
    m = _mix_common(x_ref, g_ref, w_ref, qg_ref, kg_ref, wg2_ref, bg_ref, lng_ref, lnb_ref, hsum_ref)
    q_o[...] = (m["q"] * LOG2E).astype(BF16)
    kf_o[...] = m["k"].T
    kb_o[...] = m["k"].astype(BF16)
    vf_o[...] = m["v"].T
    vb_o[...] = m["v"].astype(BF16)
    gv_o[...] = m["gv"].astype(BF16)
    sg_o[...] = m["sg"]
    la_hi, la_lo = _split_bf16(m["la"])
    b = _dot(cum_ref[...], la_hi) + _dot(cum_ref[...], la_lo)
    b_last = _dot(tot_ref[...], la_hi) + _dot(tot_ref[...], la_lo)
    eb = jnp.exp(b)
    eb_o[...] = eb
    qt_o[...] = (m["gq"] * eb).astype(BF16)
    kt_o[...] = (m["gk"] * jnp.exp(-b)).astype(BF16)
    kd_o[...] = (m["gk"] * jnp.exp(b_last - b)).astype(BF16)
    row = lax.broadcasted_iota(jnp.int32, (CM_CHUNK, CM_CHUNK), 0)
    col = lax.broadcasted_iota(jnp.int32, (CM_CHUNK, CM_CHUNK), 1)
    group = lax.broadcasted_iota(jnp.int32, (1, CM_WIDTH), 1) // CM_GROUP_DIM
    w_tril = [jnp.where(row >= col, ws_ref[g], 0.0).astype(BF16) for g in range(CM_GROUPS)]
    cv = m["cv"].astype(BF16)
    for c in range(tm // CM_CHUNK):
        rows = slice(c * CM_CHUNK, (c + 1) * CM_CHUNK)
        mixed = bs_ref[...]
        for g in range(CM_GROUPS):
            mixed = mixed + _dot(w_tril[g], jnp.where(group == g, cv[rows], jnp.zeros_like(cv[rows])))
        ocm_o[rows, :] = (m["cu"][rows] * mixed).astype(BF16)


def _mix_sample_kernel(x_ref, g_ref, w_ref, qg_ref, kg_ref, wg2_ref, bg_ref, lng_ref, lnb_ref, hsum_ref,
                       w00_ref, b0_ref,
                       q_o, k_o, v_o, gq_o, gk_o, gv_o, a_o, sg_o, cv_o, ocm_o):
    m = _mix_common(x_ref, g_ref, w_ref, qg_ref, kg_ref, wg2_ref, bg_ref, lng_ref, lnb_ref, hsum_ref)
    q_o[...] = m["q"]
    k_o[...] = m["k"]
    v_o[...] = m["v"]
    gq_o[...] = m["gq"]
    gk_o[...] = m["gk"]
    gv_o[...] = m["gv"]
    a_o[...] = jnp.exp(m["la"])
    sg_o[...] = m["sg"]
    cv_o[...] = m["cv"]
    ocm_o[...] = (m["cu"] * (w00_ref[...] * m["cv"] + b0_ref[...])).astype(BF16)


def _mix_weight_specs():
    return [
        _const_spec((1, D_MODEL)),
        _const_spec((D_MODEL, N_IN_PAD)),
        _const_spec((1, SB_WIDTH)),
        _const_spec((1, SB_WIDTH)),
        _const_spec((LANES, GLA_KEY_WIDTH)),
        _const_spec((1, GLA_KEY_WIDTH)),
        _const_spec((1, CM_WIDTH)),
        _const_spec((1, CM_WIDTH)),
        _const_spec((SB_WIDTH, SB_WIDTH)),
    ]


def _mix_prompt(x, mw, cum, tot, w_s, bs_rows, *, tm, n, L):
    t = x.shape[0]
    nt = L // tm
    kv_spec = pl.BlockSpec((None, SB_WIDTH, tm), lambda i: (i // nt, 0, i % nt))
    kv_sds = jax.ShapeDtypeStruct((n, SB_WIDTH, L), F32)
    row = lambda w: pl.BlockSpec((tm, w), lambda i: (i, 0))
    sds = lambda w, dt: jax.ShapeDtypeStruct((t, w), dt)
    widths = [(SB_WIDTH, BF16), (SB_WIDTH, F32), (SB_WIDTH, F32), (SB_WIDTH, BF16), (SB_WIDTH, BF16),
              (GLA_KEY_WIDTH, BF16), (GLA_KEY_WIDTH, BF16), (GLA_KEY_WIDTH, BF16), (GLA_WIDTH, BF16),
              (GLA_KEY_WIDTH, F32), (GLA_WIDTH, F32), (CM_WIDTH, BF16)]
    return pl.pallas_call(
        functools.partial(_mix_prompt_kernel, tm=tm),
        grid=(t // tm,),
        in_specs=[row(D_MODEL)] + _mix_weight_specs() + [
            _const_spec((tm, tm)), _const_spec((tm, tm)),
            _const_spec((CM_GROUPS, CM_CHUNK, CM_CHUNK)), _const_spec((CM_CHUNK, CM_WIDTH))],
        out_specs=[kv_spec if i in (1, 2) else row(w) for i, (w, _) in enumerate(widths)],
        out_shape=[kv_sds if i in (1, 2) else sds(w, dt) for i, (w, dt) in enumerate(widths)],
        compiler_params=_params("parallel"),
        name="mix_prompt",
    )(x, *mw, cum, tot, w_s, bs_rows)


def _mix_sample(x, mw, w00_row, b0_row):
    t = x.shape[0]
    full = lambda w: _const_spec((t, w))
    widths = [(SB_WIDTH, F32), (SB_WIDTH, F32), (SB_WIDTH, F32), (GLA_KEY_WIDTH, F32), (GLA_KEY_WIDTH, F32),
              (GLA_WIDTH, F32), (GLA_KEY_WIDTH, F32), (GLA_WIDTH, F32), (CM_WIDTH, F32), (CM_WIDTH, BF16)]
    return pl.pallas_call(
        _mix_sample_kernel,
        grid=(1,),
        in_specs=[full(D_MODEL)] + _mix_weight_specs() + [_const_spec((1, CM_WIDTH)), _const_spec((1, CM_WIDTH))],
        out_specs=[full(w) for w, _ in widths],
        out_shape=[jax.ShapeDtypeStruct((t, w), dt) for w, dt in widths],
        compiler_params=_params("arbitrary"),
        name="mix_sample",
    )(x, *mw, w00_row, b0_row)


def _sb_prompt_kernel(bias_ref, q_ref, k_ref, v_ref, u_ref, o_ref, acc_ref, z_ref, w_ref, c_ref, r_ref, *, tq):
    hp = pl.program_id(1)
    qi = pl.program_id(2)
    lane_head = lax.broadcasted_iota(jnp.int32, (1, LANES), 1) // HEAD_DIM
    q = q_ref[...]
    lane = lax.broadcasted_iota(jnp.int32, (1, LANES), 1)
    q_heads, k_fill = [], []
    for hh in range(2):
        free = (1 - hh) * HEAD_DIM
        rest = jnp.full((1, LANES), bias_ref[2 * hp + hh] * LOG2E, F32)
        parts = jnp.zeros((1, LANES), F32)
        for i in range(3):
            part = rest.astype(BF16).astype(F32)
            parts = jnp.where(lane == free + i, part, parts)
            rest = rest - part
        fill = jnp.broadcast_to(parts, (tq, LANES)).astype(BF16)
        q_heads.append(jnp.where(lane_head == hh, q, fill))
        ones = jnp.where(jnp.logical_and(lane >= free, lane < free + 3), 1.0, 0.0)
        k_fill.append(jnp.broadcast_to(ones, (tq, LANES)).astype(BF16))
    row = lax.broadcasted_iota(jnp.int32, (tq, tq), 0)
    col = lax.broadcasted_iota(jnp.int32, (tq, tq), 1)
    heads = [slice(hh * tq, (hh + 1) * tq) for hh in range(2)]

    def scores(j):
        kj = k_ref[pl.ds(pl.multiple_of(j * tq, tq), tq), :]
        return jnp.concatenate(
            [lax.dot_general(q_heads[hh], jnp.where(lane_head == hh, kj, k_fill[hh]), NT,
                             preferred_element_type=F32) for hh in range(2)], axis=0)

    def log_terms(p, diagonal):
        for hh in range(2):
            z = z_ref[p, heads[hh], :]
            sp = _softplus2(z)
            if diagonal:
                sp = jnp.where(col < row, sp, 0.0)
            w = z - sp - _dot(sp.astype(BF16), u_ref[...])
            if diagonal:
                w = jnp.where(col < row, w, MASKED_LOG)
            w_ref[p, heads[hh], :] = w
            r_ref[heads[hh], :] = jnp.broadcast_to(jnp.sum(sp, axis=1, keepdims=True), (tq, LANES))

    def attend(j, p):
        vj = v_ref[pl.ds(pl.multiple_of(j * tq, tq), tq), :]
        a = []
        for hh in range(2):
            c = c_ref[p, heads[hh], :]
            a.append(jnp.exp2(w_ref[p, heads[hh], :] + jnp.concatenate([c] * (tq // LANES), axis=1)).astype(BF16))
        o2 = _dot(jnp.concatenate(a, axis=0), vj)
        acc_ref[...] += jnp.where(lane_head == 0, o2[:tq], o2[tq:])

    def step(n, p, with_attend=True):
        if with_attend:
            attend(qi - n + 2, p)
        c_ref[p] = c_ref[1 - p] - r_ref[...]
        z_ref[1 - p] = scores(jnp.maximum(qi - n - 1, 0))
        log_terms(p, False)

    acc_ref[...] = jnp.zeros_like(acc_ref)
    c_ref[0] = jnp.zeros(c_ref.shape[1:], F32)
    z_ref[0] = scores(qi)
    log_terms(0, True)
    z_ref[1] = scores(jnp.maximum(qi - 1, 0))

    @pl.when(qi >= 1)
    def _():
        step(1, 1, with_attend=False)

    left = jnp.maximum(qi - 1, 0)

    def body(i, _):
        for k in range(UNROLL):
            step(2 + UNROLL * i + k, k % 2)
        return 0

    lax.fori_loop(0, left // UNROLL, body, 0)
    nxt = 2 + (left // UNROLL) * UNROLL
    rem = left % UNROLL

    @pl.when(rem >= 2)
    def _():
        step(nxt, 0)
        step(nxt + 1, 1)

    @pl.when(rem % 2 == 1)
    def _():
        step(qi, 0)

    @pl.when(qi >= 1)
    def _():
        attend(1, (qi - 1) % 2)

    attend(0, qi % 2)
    o_ref[...] = acc_ref[...].astype(BF16)


def _sb_prompt(q, k, v, bias, u, *, tq):
    n, L, _ = q.shape
    blk = lambda rows: (None, rows, LANES)
    return pl.pallas_call(
        functools.partial(_sb_prompt_kernel, tq=tq),
        grid=(n, SB_WIDTH // LANES, L // tq),
        in_specs=[
            pl.BlockSpec(memory_space=pltpu.SMEM),
            pl.BlockSpec(blk(tq), lambda b, h, i: (b, i, h)),
            pl.BlockSpec(blk(L), lambda b, h, i: (b, 0, h)),
            pl.BlockSpec(blk(L), lambda b, h, i: (b, 0, h)),
            _const_spec((tq, tq)),
        ],
        out_specs=pl.BlockSpec(blk(tq), lambda b, h, i: (b, i, h)),
        out_shape=jax.ShapeDtypeStruct((n, L, SB_WIDTH), BF16),
        scratch_shapes=[pltpu.VMEM((tq, LANES), F32), pltpu.VMEM((2, 2 * tq, tq), F32),
                        pltpu.VMEM((2, 2 * tq, tq), F32), pltpu.VMEM((2, 2 * tq, LANES), F32),
                        pltpu.VMEM((2 * tq, LANES), F32)],
        compiler_params=_params("parallel", "parallel", "arbitrary"),
        name="sb_prompt",
    )(bias, q, k, v, u)


def _gla_prompt_kernel(qt_ref, kt_ref, kd_ref, v_ref, eb_ref, sg_ref, gn_ref, hsum_ref,
                       og_ref, st_ref, s_ref, o_ref, *, tc):
    @pl.when(pl.program_id(1) == 0)
    def _():
        s_ref[...] = jnp.zeros_like(s_ref)

    ck = GLA_CHUNK
    key_head = lax.broadcasted_iota(jnp.int32, (1, GLA_KEY_WIDTH), 1) // GLA_DK
    val_head = lax.broadcasted_iota(jnp.int32, (1, GLA_WIDTH), 1) // GLA_DV
    state_mask = (lax.broadcasted_iota(jnp.int32, (GLA_WIDTH, GLA_KEY_WIDTH), 0) // GLA_DV
                  == lax.broadcasted_iota(jnp.int32, (GLA_WIDTH, GLA_KEY_WIDTH), 1) // GLA_DK)
    causal = (lax.broadcasted_iota(jnp.int32, (ck, ck), 0) >= lax.broadcasted_iota(jnp.int32, (ck, ck), 1))
    for c in range(tc // ck):
        rows = slice(c * ck, (c + 1) * ck)
        qt, kt, kd, v = qt_ref[rows, :], kt_ref[rows, :], kd_ref[rows, :], v_ref[rows, :]
        att = []
        v_bd = []
        for hh in range(GLA_HEADS):
            qh = jnp.where(key_head == hh, qt, jnp.zeros_like(qt))
            s = lax.dot_general(qh, kt, NT, preferred_element_type=F32)
            att.append(jnp.where(causal, s, 0.0).astype(BF16))
            v_bd.append(jnp.where(val_head == hh, v, jnp.zeros_like(v)))
        st = s_ref[...]
        o = _dot(jnp.concatenate(att, axis=1), jnp.concatenate(v_bd, axis=0))
        o = o + lax.dot_general(qt, st.astype(BF16), NT, preferred_element_type=F32)
        o_ref[rows, :] = o
        ds = lax.dot_general(v, kd, TN, preferred_element_type=F32)
        decay = eb_ref[c * ck + ck - 1:c * ck + ck, :]
        s_ref[...] = decay * st + jnp.where(state_mask, ds, 0.0)
    o = o_ref[...]
    ms = _dot((o * o).astype(BF16), hsum_ref[...]) * (1.0 / GLA_DV)
    og_ref[...] = (o * lax.rsqrt(ms + EPS) * gn_ref[...] * sg_ref[...]).astype(BF16)
    st_ref[...] = s_ref[...]


def _gla_prompt(qt, kt, kd, gv, eb, sg, gn_row, hsum, *, n, L, tc):
    t = n * L
    nc = L // tc
    row = lambda w: pl.BlockSpec((tc, w), lambda b, i: (b * nc + i, 0))
    return pl.pallas_call(
        functools.partial(_gla_prompt_kernel, tc=tc),
        grid=(n, nc),
        in_specs=[row(GLA_KEY_WIDTH), row(GLA_KEY_WIDTH), row(GLA_KEY_WIDTH), row(GLA_WIDTH),
                  row(GLA_KEY_WIDTH), row(GLA_WIDTH), _const_spec((1, GLA_WIDTH)),
                  _const_spec((GLA_WIDTH, GLA_WIDTH))],
        out_specs=[row(GLA_WIDTH), pl.BlockSpec((None, GLA_WIDTH, GLA_KEY_WIDTH), lambda b, i: (b, 0, 0))],
        out_shape=[jax.ShapeDtypeStruct((t, GLA_WIDTH), BF16),
                   jax.ShapeDtypeStruct((n, GLA_WIDTH, GLA_KEY_WIDTH), F32)],
        scratch_shapes=[pltpu.VMEM((GLA_WIDTH, GLA_KEY_WIDTH), F32), pltpu.VMEM((tc, GLA_WIDTH), F32)],
        compiler_params=_params("parallel", "arbitrary"),
        name="gla_prompt",
    )(qt, kt, kd, gv, eb, sg, gn_row, hsum)


def _sb_sample_kernel(pt_ref, q_ref, kown_ref, vown_ref, bias_ref, u_ref, *refs, pages, past):
    k_refs, v_refs = refs[:pages], refs[pages:2 * pages]
    o_ref, qb_ref, acc_ref, carry_ref = refs[2 * pages:]
    s = pl.program_id(1)
    bias = bias_ref[...]

    @pl.when(s == 0)
    def _():
        q = q_ref[...]
        qb_ref[...] = jnp.broadcast_to(q, qb_ref.shape)
        key_pos = past + 0 * lax.broadcasted_iota(jnp.int32, (SB_HEADS, 1), 0)
        valid = key_pos < past
        z = jnp.sum(q * kown_ref[...], axis=1) + bias
        sp = jnp.where(valid, _softplus(z), 0.0)
        a = jnp.where(valid, jnp.exp(z - sp), 0.0)
        lane = lax.broadcasted_iota(jnp.int32, acc_ref.shape, 2)
        own = jnp.broadcast_to(a[:, :, None] * vown_ref[...], acc_ref.shape)
        acc_ref[...] = jnp.where(lane == 0, own, 0.0)
        carry_ref[...] = -sp

    order = list(reversed(range(pages)))
    qb = qb_ref[...]
    z = jnp.concatenate([jnp.sum(k_refs[p][...] * qb, axis=1) for p in order], axis=0)
    z = z + jnp.concatenate([bias] * pages, axis=0)
    sp = _softplus(z)
    hi = sp.astype(BF16).astype(F32)
    lo = (sp - hi).astype(BF16).astype(F32)
    later = _dot(hi, u_ref[...]) + _dot(lo, u_ref[...])
    totals = jnp.sum(sp, axis=1, keepdims=True)
    carries = [carry_ref[...]]
    for i in range(pages):
        carries.append(carries[-1] - totals[i * SB_HEADS:(i + 1) * SB_HEADS])
    carry_ref[...] = carries[-1]
    a = jnp.exp(z - sp - later + jnp.concatenate(carries[:-1], axis=0))
    for h in range(SB_HEADS):
        acc = acc_ref[h]
        for i, p in enumerate(order):
            acc = acc + a[i * SB_HEADS + h:i * SB_HEADS + h + 1, :] * v_refs[p][h]
        acc_ref[h] = acc

    @pl.when(s == pl.num_programs(1) - 1)
    def _():
        o_ref[...] = jnp.sum(acc_ref[...], axis=2, keepdims=True)


def _sb_sample(page_table, q, k_own, v_own, bias_col, u, cache_kt, cache_vt, *, layer, pages):
    nb, n_pages = page_table.shape
    n_steps = n_pages // pages
    past = n_pages * PAGE_SIZE
    page_blk = (None, None, SB_HEADS, HEAD_DIM, PAGE_SIZE)

    def page_spec(p):
        return pl.BlockSpec(page_blk, lambda b, s, pt: (layer, pt[b, (n_steps - 1 - s) * pages + p], 0, 0, 0))

    per_b = pl.BlockSpec((None, SB_HEADS, HEAD_DIM, 1), lambda b, s, pt: (b, 0, 0, 0))
    grid_spec = pltpu.PrefetchScalarGridSpec(
        num_scalar_prefetch=1,
        grid=(nb, n_steps),
        in_specs=[per_b, per_b, per_b,
                  pl.BlockSpec((SB_HEADS, 1), lambda b, s, pt: (0, 0)),
                  pl.BlockSpec((PAGE_SIZE, PAGE_SIZE), lambda b, s, pt: (0, 0))]
                 + [page_spec(p) for p in range(pages)] * 2,
        out_specs=per_b,
        scratch_shapes=[pltpu.VMEM((SB_HEADS, HEAD_DIM, PAGE_SIZE), F32),
                        pltpu.VMEM((SB_HEADS, HEAD_DIM, PAGE_SIZE), F32),
                        pltpu.VMEM((SB_HEADS, 1), F32)],
    )
    return pl.pallas_call(
        functools.partial(_sb_sample_kernel, pages=pages, past=past),
        grid_spec=grid_spec,
        out_shape=jax.ShapeDtypeStruct((nb, SB_HEADS, HEAD_DIM, 1), F32),
        compiler_params=_params("parallel", "arbitrary"),
        name="sb_sample",
    )(page_table, q, k_own, v_own, bias_col, u, *([cache_kt] * pages), *([cache_vt] * pages))


def _gla_sample_kernel(a_ref, s_ref, k_ref, v_ref, q_ref, sg_ref, gn_ref, s_out, og_out):
    s_new = a_ref[...] * s_ref[...] + k_ref[...] * v_ref[...]
    s_out[...] = s_new
    o = jnp.sum(q_ref[...] * s_new, axis=2)
    og_out[...] = _rms(o, gn_ref[...]) * sg_ref[...]


def _gla_sample(a, s0, gk, gv, gq, sg, gn):
    nb = s0.shape[0]
    col = lambda t: t.reshape(nb, GLA_HEADS, GLA_DK, 1)
    args = (col(a), s0, col(gk), gv.reshape(nb, GLA_HEADS, 1, GLA_DV), col(gq),
            sg.reshape(nb, GLA_HEADS, GLA_DV), gn.reshape(1, 1, GLA_DV))
    return pl.pallas_call(
        _gla_sample_kernel,
        out_shape=[jax.ShapeDtypeStruct(s0.shape, F32), jax.ShapeDtypeStruct((nb, GLA_HEADS, GLA_DV), F32)],
        name="gla_sample",
    )(*args)


def _tile(t, target):
    return target if t % target == 0 else t


def kernel(x_prompt, x_sample, cache_k, cache_v, state_gla, page_table, ffn1_norm_g, ffn1_w_up, ffn1_w_down, mix_norm_g, w_in, q_norm_g, k_norm_g, sb_logit_bias, gla_w_gate2, gla_b_gate, gla_out_norm_g, cm_ln_g, cm_ln_b, cm_w_spatial, cm_b_spatial, w_out, ffn2_norm_g, ffn2_w_up, ffn2_w_down):
    nb_p, L, d = x_prompt.shape
    nb_s, n_dec, _ = x_sample.shape
    depth = w_in.shape[0]
    assert n_dec == 1 and d == D_MODEL and L % CM_CHUNK == 0
    tp = nb_p * L
    tm = _tile(L, 512)
    tq = _tile(L, 256)
    tf = D_FF // 2
    pages = 8 if page_table.shape[1] % 8 == 0 else 1

    idx = jnp.arange(tm)
    same_chunk = (idx[:, None] // GLA_CHUNK) == (idx[None, :] // GLA_CHUNK)
    cum = (same_chunk & (idx[None, :] <= idx[:, None])).astype(BF16)
    tot = same_chunk.astype(BF16)
    hsum512 = ((jnp.arange(SB_WIDTH)[:, None] // HEAD_DIM) == (jnp.arange(SB_WIDTH)[None, :] // HEAD_DIM)).astype(BF16)
    hsum256 = hsum512[:GLA_WIDTH, :GLA_WIDTH]
    iq = jnp.arange(tq)
    u_later = (iq[:, None] > iq[None, :]).astype(BF16)
    ip = jnp.arange(PAGE_SIZE)
    u_page = (ip[:, None] > ip[None, :]).astype(F32)
    cache_kt = jnp.transpose(cache_k, (0, 1, 3, 4, 2))
    cache_vt = jnp.transpose(cache_v, (0, 1, 3, 4, 2))

    xp = x_prompt.reshape(tp, d)
    xs = x_sample.reshape(nb_s * n_dec, d)
    row = lambda v: v.reshape(1, -1).astype(F32)
    outs = {k: [] for k in ("kp", "vp", "sp", "ks", "vs", "ss", "cv")}
    for l in range(depth):
        w = w_in[l]
        w_r = jnp.concatenate([w[:, :C_GG], w[:, C_GG + GLA_RANK:], w[:, C_GG:C_GG + GLA_RANK],
                               jnp.zeros((d, N_IN_PAD - w.shape[1]), w.dtype)], axis=1).astype(BF16)
        wg2 = jnp.concatenate([gla_w_gate2[l], jnp.zeros((LANES - GLA_RANK, GLA_KEY_WIDTH), F32)], axis=0)
        mw = (row(mix_norm_g[l]), w_r, row(jnp.tile(q_norm_g[l], SB_HEADS)), row(jnp.tile(k_norm_g[l], SB_HEADS)),
              wg2, row(gla_b_gate[l]), row(cm_ln_g[l]), row(cm_ln_b[l]), hsum512)
        f1 = (row(ffn1_norm_g[l]), ffn1_w_up[l].astype(BF16), ffn1_w_down[l].astype(BF16))
        f2 = (row(ffn2_norm_g[l]), ffn2_w_up[l].astype(BF16), ffn2_w_down[l].astype(BF16))
        wo = w_out[l].astype(BF16)
        gn_row = row(jnp.tile(gla_out_norm_g[l], GLA_HEADS))
        bs_rows = jnp.repeat(cm_b_spatial[l].T, CM_GROUP_DIM, axis=1)

        xp = _ffn(xp, *f1, tm=tm, tf=tf)
        (q, kf, vf, kb, vb, qt, kt, kd, gv, eb, sg, ocm) = _mix_prompt(xp, mw, cum, tot, cm_w_spatial[l], bs_rows, tm=tm,
                                                                         n=nb_p, L=L)
        r3 = lambda t: t.reshape(nb_p, L, SB_WIDTH)
        o_sb = _sb_prompt(r3(q), r3(kb), r3(vb), sb_logit_bias[l], u_later, tq=tq).reshape(tp, SB_WIDTH)
        o_gla, st = _gla_prompt(qt, kt, kd, gv, eb, sg, gn_row, hsum256, n=nb_p, L=L, tc=tm)
        xp = _ffn(xp, *f2, tm=tm, tf=tf, merge=(o_sb, o_gla, ocm, wo))
        to_cache = lambda t: t.reshape(nb_p, SB_HEADS, HEAD_DIM, L).transpose(0, 3, 1, 2)
        outs["kp"].append(to_cache(kf))
        outs["vp"].append(to_cache(vf))
        st5 = st.reshape(nb_p, GLA_HEADS, GLA_DV, GLA_HEADS, GLA_DK)
        outs["sp"].append(jnp.stack([st5[:, h, :, h, :] for h in range(GLA_HEADS)], axis=1).transpose(0, 1, 3, 2))

        ts = xs.shape[0]
        xs = _ffn(xs, *f1, tm=ts, tf=tf)
        w00_row = row(jnp.repeat(cm_w_spatial[l][:, 0, 0], CM_GROUP_DIM))
        b0_row = row(jnp.repeat(cm_b_spatial[l][:, 0], CM_GROUP_DIM))
        (q, k, v, gq, gk, gv, a, sg, cv, ocm) = _mix_sample(xs, mw, w00_row, b0_row)
        col4 = lambda t: t.reshape(nb_s, SB_HEADS, HEAD_DIM, 1)
        o_sb = _sb_sample(page_table, col4(q), col4(k), col4(v), sb_logit_bias[l].reshape(SB_HEADS, 1), u_page,
                          cache_kt, cache_vt, layer=l, pages=pages).reshape(ts, SB_WIDTH)
        s_new, og = _gla_sample(a, state_gla[l], gk, gv, gq, sg, gla_out_norm_g[l])
        xs = _ffn(xs, *f2, tm=ts, tf=tf,
                  merge=(o_sb.astype(BF16), og.reshape(ts, GLA_WIDTH).astype(BF16), ocm, wo))
        outs["ks"].append(k.reshape(nb_s, n_dec, SB_HEADS, HEAD_DIM))
        outs["vs"].append(v.reshape(nb_s, n_dec, SB_HEADS, HEAD_DIM))
        outs["ss"].append(s_new)
        outs["cv"].append(cv.reshape(nb_s, n_dec, CM_GROUPS, CM_GROUP_DIM))

    st = lambda k: jnp.stack(outs[k])
    return (xp.reshape(nb_p, L, d), xs.reshape(nb_s, n_dec, d), st("kp"), st("vp"), st("sp"),
            st("ks"), st("vs"), st("ss"), st("cv"))
```

```python
import functools

import jax
import jax.numpy as jnp
from jax import lax
from jax.experimental import pallas as pl
from jax.experimental.pallas import tpu as pltpu

F32 = jnp.float32
BF16 = jnp.bfloat16

D_MODEL = 1024
HEAD_DIM = 64
SB_HEADS = 8
SB_WIDTH = SB_HEADS * HEAD_DIM
GLA_HEADS = 4
GLA_DV = 64
GLA_DK = 32
GLA_WIDTH = GLA_HEADS * GLA_DV
GLA_KEY_WIDTH = GLA_HEADS * GLA_DK
GLA_RANK = 16
GLA_TAU = 16.0
GLA_CHUNK = 64
CM_GROUPS = 4
CM_WIDTH = 256
CM_GROUP_DIM = 64
CM_CHUNK = 128
D_FF = 2816
EPS = 1e-6
PAGE_SIZE = 128

LOG2E = 1.4426950408889634
MASKED_LOG = -1e30
UNROLL = 4
LANES = 128
VMEM_LIMIT = 56 * 1024 * 1024

C_SQ, C_SK, C_SV = 0, 512, 1024
C_GQ, C_GK, C_GV, C_GG = 1536, 1664, 1792, 2048
C_CU, C_CV, C_LR = 2304, 2560, 2816
N_IN_PAD = 2944

NT = (((1,), (1,)), ((), ()))
TN = (((0,), (0,)), ((), ()))


def _params(*sem):
    return pltpu.CompilerParams(dimension_semantics=sem, vmem_limit_bytes=VMEM_LIMIT)


def _const_spec(shape):
    n = len(shape)
    return pl.BlockSpec(shape, lambda *_: (0,) * n)


def _softplus(z):
    return jnp.maximum(z, 0.0) + jnp.log1p(jnp.exp(-jnp.abs(z)))


def _softplus2(x):
    neg_abs = pltpu.bitcast(pltpu.bitcast(x, jnp.uint32) | jnp.uint32(0x80000000), F32)
    return jnp.maximum(x, 0.0) + jnp.log2(1.0 + jnp.exp2(neg_abs))


def _split_bf16(x):
    hi = x.astype(BF16)
    lo = (x - hi.astype(F32)).astype(BF16)
    return hi, lo


def _dot(a, b):
    return jnp.dot(a, b, preferred_element_type=F32)


def _rms(x, g_row):
    return x * lax.rsqrt(jnp.mean(x * x, axis=-1, keepdims=True) + EPS) * g_row


def _ffn_kernel(*refs, nf, merged):
    if merged:
        x_ref, sb_ref, gla_ref, cm_ref, wo_ref, g_ref, wg_ref, wu_ref, wd_ref, o_ref, h_ref, acc_ref, x1_ref = refs
    else:
        x_ref, g_ref, wg_ref, wu_ref, wd_ref, o_ref, h_ref, acc_ref = refs
        x1_ref = x_ref
    f = pl.program_id(1)

    @pl.when(f == 0)
    def _():
        if merged:
            o = _dot(sb_ref[...], wo_ref[0:SB_WIDTH, :])
            o = o + _dot(gla_ref[...], wo_ref[SB_WIDTH:SB_WIDTH + GLA_WIDTH, :])
            o = o + _dot(cm_ref[...], wo_ref[SB_WIDTH + GLA_WIDTH:, :])
            x1_ref[...] = x_ref[...] + o
        h_ref[...] = _rms(x1_ref[...], g_ref[...]).astype(BF16)
        acc_ref[...] = jnp.zeros_like(acc_ref)

    h = h_ref[...]
    gate = _dot(h, wg_ref[...])
    up = _dot(h, wu_ref[...])
    act = (gate * jax.nn.sigmoid(gate) * up).astype(BF16)
    acc_ref[...] += _dot(act, wd_ref[...])

    @pl.when(f == nf - 1)
    def _():
        o_ref[...] = x1_ref[...] + 0.5 * acc_ref[...]


def _ffn(x, g_row, w_up, w_down, *, tm, tf, merge=None):
    t, d = x.shape
    nf = D_FF // tf
    row = lambda w: pl.BlockSpec((tm, w), lambda i, f: (i, 0))
    merge_specs, merge_args, scratch = [], (), []
    if merge is not None:
        merge_specs = [row(SB_WIDTH), row(GLA_WIDTH), row(CM_WIDTH), _const_spec((d, d))]
        merge_args = tuple(merge)
        scratch = [pltpu.VMEM((tm, d), F32)]
    return pl.pallas_call(
        functools.partial(_ffn_kernel, nf=nf, merged=merge is not None),
        grid=(t // tm, nf),
        in_specs=[row(d)] + merge_specs + [
            _const_spec((1, d)),
            pl.BlockSpec((d, tf), lambda i, f: (0, f)),
            pl.BlockSpec((d, tf), lambda i, f: (0, f + nf)),
            pl.BlockSpec((tf, d), lambda i, f: (f, 0)),
        ],
        out_specs=row(d),
        out_shape=jax.ShapeDtypeStruct((t, d), F32),
        scratch_shapes=[pltpu.VMEM((tm, d), BF16), pltpu.VMEM((tm, d), F32)] + scratch,
        compiler_params=_params("parallel", "arbitrary"),
        name="half_ffn_merged" if merge is not None else "half_ffn",
    )(x, *merge_args, g_row, w_up, w_up, w_down)


def _mix_common(x_ref, g_ref, w_ref, qg_ref, kg_ref, wg2_ref, bg_ref, lng_ref, lnb_ref, hsum_ref):
    h = _rms(x_ref[...], g_ref[...]).astype(BF16)

    def proj(lo, width):
        return _dot(h, w_ref[:, lo:lo + width])

    def head_norm(t, gain_row):
        ms = _dot((t * t).astype(BF16), hsum_ref[...]) * (1.0 / HEAD_DIM)
        return t * lax.rsqrt(ms + EPS) * gain_row

    out = {}
    out["q"] = head_norm(proj(C_SQ, SB_WIDTH), qg_ref[...]) * (HEAD_DIM ** -0.5)
    out["k"] = head_norm(proj(C_SK, SB_WIDTH), kg_ref[...])
    out["v"] = proj(C_SV, SB_WIDTH)
    out["gq"] = proj(C_GQ, GLA_KEY_WIDTH) * (GLA_DK ** -0.5)
    out["gk"] = proj(C_GK, GLA_KEY_WIDTH)
    out["gv"] = proj(C_GV, GLA_WIDTH)
    gg = proj(C_GG, GLA_WIDTH)
    out["sg"] = gg * jax.nn.sigmoid(gg)
    lr_hi, lr_lo = _split_bf16(proj(C_LR, LANES))
    w_hi, w_lo = _split_bf16(wg2_ref[...])
    pre = _dot(lr_hi, w_hi) + _dot(lr_lo, w_hi) + _dot(lr_hi, w_lo) + bg_ref[...]
    out["la"] = -_softplus(-pre) * (1.0 / GLA_TAU)
    out["cu"] = jax.nn.gelu(proj(C_CU, CM_WIDTH))
    c = jax.nn.gelu(proj(C_CV, CM_WIDTH))
    c = c - jnp.mean(c, axis=-1, keepdims=True)
    out["cv"] = c * lax.rsqrt(jnp.mean(c * c, axis=-1, keepdims=True) + EPS) * lng_ref[...] + lnb_ref[...]
    return out


def _mix_prompt_kernel(x_ref, g_ref, w_ref, qg_ref, kg_ref, wg2_ref, bg_ref, lng_ref, lnb_ref, hsum_ref,
                       cum_ref, tot_ref, ws_ref, bs_ref,
                       q_o, kf_o, vf_o, kb_o, vb_o, qt_o, kt_o, kd_o, gv_o, eb_o, sg_o, ocm_o, *, tm):
    m = _mix_common(x_ref, g_ref, w_ref, qg_ref, kg_ref, wg2_ref, bg_ref, lng_ref, lnb_ref, hsum_ref)
    q_o[...] = (m["q"] * LOG2E).astype(BF16)
    kf_o[...] = m["k"].T
    kb_o[...] = m["k"].astype(BF16)
    vf_o[...] = m["v"].T
    vb_o[...] = m["v"].astype(BF16)
    gv_o[...] = m["gv"].astype(BF16)
    sg_o[...] = m["sg"]
    la_hi, la_lo = _split_bf16(m["la"])
    b = _dot(cum_ref[...], la_hi) + _dot(cum_ref[...], la_lo)
    b_last = _dot(tot_ref[...], la_hi) + _dot(tot_ref[...], la_lo)
    eb = jnp.exp(b)
    eb_o[...] = eb
    qt_o[...] = (m["gq"] * eb).astype(BF16)
    kt_o[...] = (m["gk"] * jnp.exp(-b)).astype(BF16)
    kd_o[...] = (m["gk"] * jnp.exp(b_last - b)).astype(BF16)
    row = lax.broadcasted_iota(jnp.int32, (CM_CHUNK, CM_CHUNK), 0)
    col = lax.broadcasted_iota(jnp.int32, (CM_CHUNK, CM_CHUNK), 1)
    group = lax.broadcasted_iota(jnp.int32, (1, CM_WIDTH), 1) // CM_GROUP_DIM
    w_tril = [jnp.where(row >= col, ws_ref[g], 0.0).astype(BF16) for g in range(CM_GROUPS)]
    cv = m["cv"].astype(BF16)
    for c in range(tm // CM_CHUNK):
        rows = slice(c * CM_CHUNK, (c + 1) * CM_CHUNK)
        mixed = bs_ref[...]
        for g in range(CM_GROUPS):
            mixed = mixed + _dot(w_tril[g], jnp.where(group == g, cv[rows], jnp.zeros_like(cv[rows])))
        ocm_o[rows, :] = (m["cu"][rows] * mixed).astype(BF16)


def _mix_sample_kernel(x_ref, g_ref, w_ref, qg_ref, kg_ref, wg2_ref, bg_ref, lng_ref, lnb_ref, hsum_ref,
                       w00_ref, b0_ref,
                       q_o, k_o, v_o, gq_o, gk_o, gv_o, a_o, sg_o, cv_o, ocm_o):
    m = _mix_common(x_ref, g_ref, w_ref, qg_ref, kg_ref, wg2_ref, bg_ref, lng_ref, lnb_ref, hsum_ref)
    q_o[...] = m["q"]
    k_o[...] = m["k"]
    v_o[...] = m["v"]
    gq_o[...] = m["gq"]
    gk_o[...] = m["gk"]
    gv_o[...] = m["gv"]
    a_o[...] = jnp.exp(m["la"])
    sg_o[...] = m["sg"]
    cv_o[...] = m["cv"]
    ocm_o[...] = (m["cu"] * (w00_ref[...] * m["cv"] + b0_ref[...])).astype(BF16)


def _mix_weight_specs():
    return [
        _const_spec((1, D_MODEL)),
        _const_spec((D_MODEL, N_IN_PAD)),
        _const_spec((1, SB_WIDTH)),
        _const_spec((1, SB_WIDTH)),
        _const_spec((LANES, GLA_KEY_WIDTH)),
        _const_spec((1, GLA_KEY_WIDTH)),
        _const_spec((1, CM_WIDTH)),
        _const_spec((1, CM_WIDTH)),
        _const_spec((SB_WIDTH, SB_WIDTH)),
    ]


def _mix_prompt(x, mw, cum, tot, w_s, bs_rows, *, tm, n, L):
    t = x.shape[0]
    nt = L // tm
    kv_spec = pl.BlockSpec((None, SB_WIDTH, tm), lambda i: (i // nt, 0, i % nt))
    kv_sds = jax.ShapeDtypeStruct((n, SB_WIDTH, L), F32)
    row = lambda w: pl.BlockSpec((tm, w), lambda i: (i, 0))
    sds = lambda w, dt: jax.ShapeDtypeStruct((t, w), dt)
    widths = [(SB_WIDTH, BF16), (SB_WIDTH, F32), (SB_WIDTH, F32), (SB_WIDTH, BF16), (SB_WIDTH, BF16),
              (GLA_KEY_WIDTH, BF16), (GLA_KEY_WIDTH, BF16), (GLA_KEY_WIDTH, BF16), (GLA_WIDTH, BF16),
              (GLA_KEY_WIDTH, F32), (GLA_WIDTH, F32), (CM_WIDTH, BF16)]
    return pl.pallas_call(
        functools.partial(_mix_prompt_kernel, tm=tm),
        grid=(t // tm,),
        in_specs=[row(D_MODEL)] + _mix_weight_specs() + [
            _const_spec((tm, tm)), _const_spec((tm, tm)),
            _const_spec((CM_GROUPS, CM_CHUNK, CM_CHUNK)), _const_spec((CM_CHUNK, CM_WIDTH))],
        out_specs=[kv_spec if i in (1, 2) else row(w) for i, (w, _) in enumerate(widths)],
        out_shape=[kv_sds if i in (1, 2) else sds(w, dt) for i, (w, dt) in enumerate(widths)],
        compiler_params=_params("parallel"),
        name="mix_prompt",
    )(x, *mw, cum, tot, w_s, bs_rows)


def _mix_sample(x, mw, w00_row, b0_row):
    t = x.shape[0]
    full = lambda w: _const_spec((t, w))
    widths = [(SB_WIDTH, F32), (SB_WIDTH, F32), (SB_WIDTH, F32), (GLA_KEY_WIDTH, F32), (GLA_KEY_WIDTH, F32),
              (GLA_WIDTH, F32), (GLA_KEY_WIDTH, F32), (GLA_WIDTH, F32), (CM_WIDTH, F32), (CM_WIDTH, BF16)]
    return pl.pallas_call(
        _mix_sample_kernel,
        grid=(1,),
        in_specs=[full(D_MODEL)] + _mix_weight_specs() + [_const_spec((1, CM_WIDTH)), _const_spec((1, CM_WIDTH))],
        out_specs=[full(w) for w, _ in widths],
        out_shape=[jax.ShapeDtypeStruct((t, w), dt) for w, dt in widths],
        compiler_params=_params("arbitrary"),
        name="mix_sample",
    )(x, *mw, w00_row, b0_row)


def _sb_prompt_kernel(*refs, tq, hosted):
    hp = pl.program_id(1)
    qi = pl.program_id(2)
    if hosted is None:
        bias_ref, q_ref, k_ref, v_ref, u_ref, o_ref, acc_ref, z_ref, w_ref, c_ref, r_ref = refs
    else:
        pages, steps_per_row, past = hosted
        bias_ref, q_ref, k_ref, v_ref, u_ref = refs[1:6]
        s_in = refs[6:6 + N_SAMPLE_IN]
        pages_at = 6 + N_SAMPLE_IN
        s_k, s_v = refs[pages_at:pages_at + pages], refs[pages_at + pages:pages_at + 2 * pages]
        o_ref, s_o, acc_ref, z_ref, w_ref, c_ref, r_ref, s_qb, s_acc, s_carry = refs[pages_at + 2 * pages:]
        s_step = qi % steps_per_row
        _sample_sweep_step(s_step, s_step == steps_per_row - 1, *s_in, s_k, s_v, s_o, s_qb, s_acc, s_carry,
                           past=past)
    lane_head = lax.broadcasted_iota(jnp.int32, (1, LANES), 1) // HEAD_DIM
    q = q_ref[...]
    lane = lax.broadcasted_iota(jnp.int32, (1, LANES), 1)
    q_heads, k_fill = [], []
    for hh in range(2):
        free = (1 - hh) * HEAD_DIM
        rest = jnp.full((1, LANES), bias_ref[2 * hp + hh] * LOG2E, F32)
        parts = jnp.zeros((1, LANES), F32)
        for i in range(3):
            part = rest.astype(BF16).astype(F32)
            parts = jnp.where(lane == free + i, part, parts)
            rest = rest - part
        fill = jnp.broadcast_to(parts, (tq, LANES)).astype(BF16)
        q_heads.append(jnp.where(lane_head == hh, q, fill))
        ones = jnp.where(jnp.logical_and(lane >= free, lane < free + 3), 1.0, 0.0)
        k_fill.append(jnp.broadcast_to(ones, (tq, LANES)).astype(BF16))
    row = lax.broadcasted_iota(jnp.int32, (tq, tq), 0)
    col = lax.broadcasted_iota(jnp.int32, (tq, tq), 1)
    heads = [slice(hh * tq, (hh + 1) * tq) for hh in range(2)]

    def scores(j):
        kj = k_ref[pl.ds(pl.multiple_of(j * tq, tq), tq), :]
        return jnp.concatenate(
            [lax.dot_general(q_heads[hh], jnp.where(lane_head == hh, kj, k_fill[hh]), NT,
                             preferred_element_type=F32) for hh in range(2)], axis=0)

    def log_terms(p, diagonal):
        for hh in range(2):
            z = z_ref[p, heads[hh], :]
            sp = _softplus2(z)
            if diagonal:
                sp = jnp.where(col < row, sp, 0.0)
            w = z - sp - _dot(sp.astype(BF16), u_ref[...])
            if diagonal:
                w = jnp.where(col < row, w, MASKED_LOG)
            w_ref[p, heads[hh], :] = w
            r_ref[heads[hh], :] = jnp.broadcast_to(jnp.sum(sp, axis=1, keepdims=True), (tq, LANES))

    def attend(j, p):
        vj = v_ref[pl.ds(pl.multiple_of(j * tq, tq), tq), :]
        a = []
        for hh in range(2):
            c = c_ref[p, heads[hh], :]
            a.append(jnp.exp2(w_ref[p, heads[hh], :] + jnp.concatenate([c] * (tq // LANES), axis=1)).astype(BF16))
        o2 = _dot(jnp.concatenate(a, axis=0), vj)
        acc_ref[...] += jnp.where(lane_head == 0, o2[:tq], o2[tq:])

    def step(n, p, with_attend=True):
        if with_attend:
            attend(qi - n + 2, p)
        c_ref[p] = c_ref[1 - p] - r_ref[...]
        z_ref[1 - p] = scores(jnp.maximum(qi - n - 1, 0))
        log_terms(p, False)

    acc_ref[...] = jnp.zeros_like(acc_ref)
    c_ref[0] = jnp.zeros(c_ref.shape[1:], F32)
    z_ref[0] = scores(qi)
    log_terms(0, True)
    z_ref[1] = scores(jnp.maximum(qi - 1, 0))

    @pl.when(qi >= 1)
    def _():
        step(1, 1, with_attend=False)

    left = jnp.maximum(qi - 1, 0)

    def body(i, _):
        for k in range(UNROLL):
            step(2 + UNROLL * i + k, k % 2)
        return 0

    lax.fori_loop(0, left // UNROLL, body, 0)
    nxt = 2 + (left // UNROLL) * UNROLL
    rem = left % UNROLL

    @pl.when(rem >= 2)
    def _():
        step(nxt, 0)
        step(nxt + 1, 1)

    @pl.when(rem % 2 == 1)
    def _():
        step(qi, 0)

    @pl.when(qi >= 1)
    def _():
        attend(1, (qi - 1) % 2)

    attend(0, qi % 2)
    o_ref[...] = acc_ref[...].astype(BF16)


def _sb_hosting_fits(n, L, tq, page_table, pages):
    nb, n_pages = page_table.shape
    nq, steps_per_row = L // tq, n_pages // pages
    return n_pages % pages == 0 and nq % steps_per_row == 0 and n * (SB_WIDTH // LANES) * (nq // steps_per_row) == nb


def _sb_prompt(q, k, v, bias, u, *, tq, sample=None):
    n, L, _ = q.shape
    nh, nq = SB_WIDTH // LANES, L // tq
    blk = lambda rows: (None, rows, LANES)
    in_specs = [
        pl.BlockSpec(memory_space=pltpu.SMEM),
        pl.BlockSpec(blk(tq), lambda b, h, i, *_: (b, i, h)),
        pl.BlockSpec(blk(L), lambda b, h, i, *_: (b, 0, h)),
        pl.BlockSpec(blk(L), lambda b, h, i, *_: (b, 0, h)),
        pl.BlockSpec((tq, tq), lambda b, h, i, *_: (0, 0)),
    ]
    out_specs = pl.BlockSpec(blk(tq), lambda b, h, i, *_: (b, i, h))
    out_shape = jax.ShapeDtypeStruct((n, L, SB_WIDTH), BF16)
    scratch = [pltpu.VMEM((tq, LANES), F32), pltpu.VMEM((2, 2 * tq, tq), F32), pltpu.VMEM((2, 2 * tq, tq), F32),
               pltpu.VMEM((2, 2 * tq, LANES), F32), pltpu.VMEM((2 * tq, LANES), F32)]
    if sample is None:
        return pl.pallas_call(
            functools.partial(_sb_prompt_kernel, tq=tq, hosted=None),
            grid=(n, nh, nq), in_specs=in_specs, out_specs=out_specs, out_shape=out_shape, scratch_shapes=scratch,
            compiler_params=_params("parallel", "parallel", "arbitrary"),
            name="sb_prompt",
        )(bias, q, k, v, u)

    page_table, s_q, s_k, s_v, s_bias, s_u, cache_kt, cache_vt, layer, pages = sample
    nb, n_pages = page_table.shape
    spr = n_pages // pages
    row_of = lambda b, h, i: (b * nh + h) * (nq // spr) + i // spr

    def page_spec(p):
        return pl.BlockSpec(
            (None, None, SB_HEADS, HEAD_DIM, PAGE_SIZE),
            lambda b, h, i, pt: (layer, pt[row_of(b, h, i), (spr - 1 - i % spr) * pages + p], 0, 0, 0))

    per_row = pl.BlockSpec((None, SB_HEADS, HEAD_DIM, 1), lambda b, h, i, pt: (row_of(b, h, i), 0, 0, 0))
    grid_spec = pltpu.PrefetchScalarGridSpec(
        num_scalar_prefetch=1,
        grid=(n, nh, nq),
        in_specs=in_specs + [per_row, per_row, per_row,
                             pl.BlockSpec((SB_HEADS, 1), lambda b, h, i, pt: (0, 0)),
                             pl.BlockSpec((PAGE_SIZE, PAGE_SIZE), lambda b, h, i, pt: (0, 0))]
                 + [page_spec(p) for p in range(pages)] * 2,
        out_specs=[out_specs, per_row],
        scratch_shapes=scratch + _sample_scratch(),
    )
    return pl.pallas_call(
        functools.partial(_sb_prompt_kernel, tq=tq, hosted=(pages, spr, n_pages * PAGE_SIZE)),
        grid_spec=grid_spec,
        out_shape=[out_shape, jax.ShapeDtypeStruct((nb, SB_HEADS, HEAD_DIM, 1), F32)],
        compiler_params=_params("arbitrary", "arbitrary", "arbitrary"),
        name="sb_prompt_hosting_sample",
    )(page_table, bias, q, k, v, u, s_q, s_k, s_v, s_bias, s_u, *([cache_kt] * pages), *([cache_vt] * pages))


def _gla_prompt_kernel(qt_ref, kt_ref, kd_ref, v_ref, eb_ref, sg_ref, gn_ref, hsum_ref,
                       og_ref, st_ref, s_ref, o_ref, *, tc):
    @pl.when(pl.program_id(1) == 0)
    def _():
        s_ref[...] = jnp.zeros_like(s_ref)

    ck = GLA_CHUNK
    key_head = lax.broadcasted_iota(jnp.int32, (1, GLA_KEY_WIDTH), 1) // GLA_DK
    val_head = lax.broadcasted_iota(jnp.int32, (1, GLA_WIDTH), 1) // GLA_DV
    state_mask = (lax.broadcasted_iota(jnp.int32, (GLA_WIDTH, GLA_KEY_WIDTH), 0) // GLA_DV
                  == lax.broadcasted_iota(jnp.int32, (GLA_WIDTH, GLA_KEY_WIDTH), 1) // GLA_DK)
    causal = (lax.broadcasted_iota(jnp.int32, (ck, ck), 0) >= lax.broadcasted_iota(jnp.int32, (ck, ck), 1))
    for c in range(tc // ck):
        rows = slice(c * ck, (c + 1) * ck)
        qt, kt, kd, v = qt_ref[rows, :], kt_ref[rows, :], kd_ref[rows, :], v_ref[rows, :]
        att = []
        v_bd = []
        for hh in range(GLA_HEADS):
            qh = jnp.where(key_head == hh, qt, jnp.zeros_like(qt))
            s = lax.dot_general(qh, kt, NT, preferred_element_type=F32)
            att.append(jnp.where(causal, s, 0.0).astype(BF16))
            v_bd.append(jnp.where(val_head == hh, v, jnp.zeros_like(v)))
        st = s_ref[...]
        o = _dot(jnp.concatenate(att, axis=1), jnp.concatenate(v_bd, axis=0))
        o = o + lax.dot_general(qt, st.astype(BF16), NT, preferred_element_type=F32)
        o_ref[rows, :] = o
        ds = lax.dot_general(v, kd, TN, preferred_element_type=F32)
        decay = eb_ref[c * ck + ck - 1:c * ck + ck, :]
        s_ref[...] = decay * st + jnp.where(state_mask, ds, 0.0)
    o = o_ref[...]
    ms = _dot((o * o).astype(BF16), hsum_ref[...]) * (1.0 / GLA_DV)
    og_ref[...] = (o * lax.rsqrt(ms + EPS) * gn_ref[...] * sg_ref[...]).astype(BF16)
    st_ref[...] = s_ref[...]


def _gla_prompt(qt, kt, kd, gv, eb, sg, gn_row, hsum, *, n, L, tc):
    t = n * L
    nc = L // tc
    row = lambda w: pl.BlockSpec((tc, w), lambda b, i: (b * nc + i, 0))
    return pl.pallas_call(
        functools.partial(_gla_prompt_kernel, tc=tc),
        grid=(n, nc),
        in_specs=[row(GLA_KEY_WIDTH), row(GLA_KEY_WIDTH), row(GLA_KEY_WIDTH), row(GLA_WIDTH),
                  row(GLA_KEY_WIDTH), row(GLA_WIDTH), _const_spec((1, GLA_WIDTH)),
                  _const_spec((GLA_WIDTH, GLA_WIDTH))],
        out_specs=[row(GLA_WIDTH), pl.BlockSpec((None, GLA_WIDTH, GLA_KEY_WIDTH), lambda b, i: (b, 0, 0))],
        out_shape=[jax.ShapeDtypeStruct((t, GLA_WIDTH), BF16),
                   jax.ShapeDtypeStruct((n, GLA_WIDTH, GLA_KEY_WIDTH), F32)],
        scratch_shapes=[pltpu.VMEM((GLA_WIDTH, GLA_KEY_WIDTH), F32), pltpu.VMEM((tc, GLA_WIDTH), F32)],
        compiler_params=_params("parallel", "arbitrary"),
        name="gla_prompt",
    )(qt, kt, kd, gv, eb, sg, gn_row, hsum)


def _sample_sweep_step(s, is_last, q_ref, kown_ref, vown_ref, bias_ref, u_ref, k_refs, v_refs, o_ref,
                       qb_ref, acc_ref, carry_ref, *, past):
    pages = len(k_refs)
    bias = bias_ref[...]

    @pl.when(s == 0)
    def _():
        q = q_ref[...]
        qb_ref[...] = jnp.broadcast_to(q, qb_ref.shape)
        key_pos = past + 0 * lax.broadcasted_iota(jnp.int32, (SB_HEADS, 1), 0)
        valid = key_pos < past
        z = jnp.sum(q * kown_ref[...], axis=1) + bias
        sp = jnp.where(valid, _softplus(z), 0.0)
        a = jnp.where(valid, jnp.exp(z - sp), 0.0)
        lane = lax.broadcasted_iota(jnp.int32, acc_ref.shape, 2)
        own = jnp.broadcast_to(a[:, :, None] * vown_ref[...], acc_ref.shape)
        acc_ref[...] = jnp.where(lane == 0, own, 0.0)
        carry_ref[...] = -sp

    order = list(reversed(range(pages)))
    qb = qb_ref[...]
    z = jnp.concatenate([jnp.sum(k_refs[p][...] * qb, axis=1) for p in order], axis=0)
    z = z + jnp.concatenate([bias] * pages, axis=0)
    sp = _softplus(z)
    hi = sp.astype(BF16).astype(F32)
    lo = (sp - hi).astype(BF16).astype(F32)
    later = _dot(hi, u_ref[...]) + _dot(lo, u_ref[...])
    totals = jnp.sum(sp, axis=1, keepdims=True)
    carries = [carry_ref[...]]
    for i in range(pages):
        carries.append(carries[-1] - totals[i * SB_HEADS:(i + 1) * SB_HEADS])
    carry_ref[...] = carries[-1]
    a = jnp.exp(z - sp - later + jnp.concatenate(carries[:-1], axis=0))
    for h in range(SB_HEADS):
        acc = acc_ref[h]
        for i, p in enumerate(order):
            acc = acc + a[i * SB_HEADS + h:i * SB_HEADS + h + 1, :] * v_refs[p][h]
        acc_ref[h] = acc

    @pl.when(is_last)
    def _():
        o_ref[...] = jnp.sum(acc_ref[...], axis=2, keepdims=True)


N_SAMPLE_IN = 5


def _sample_scratch():
    return [pltpu.VMEM((SB_HEADS, HEAD_DIM, PAGE_SIZE), F32), pltpu.VMEM((SB_HEADS, HEAD_DIM, PAGE_SIZE), F32),
            pltpu.VMEM((SB_HEADS, 1), F32)]


def _sb_sample_kernel(pt_ref, *refs, pages, past):
    ins, refs = refs[:N_SAMPLE_IN], refs[N_SAMPLE_IN:]
    k_refs, v_refs = refs[:pages], refs[pages:2 * pages]
    o_ref, qb_ref, acc_ref, carry_ref = refs[2 * pages:]
    s = pl.program_id(1)
    _sample_sweep_step(s, s == pl.num_programs(1) - 1, *ins, k_refs, v_refs, o_ref, qb_ref, acc_ref, carry_ref,
                       past=past)


def _sb_sample(page_table, q, k_own, v_own, bias_col, u, cache_kt, cache_vt, *, layer, pages):
    nb, n_pages = page_table.shape
    n_steps = n_pages // pages
    past = n_pages * PAGE_SIZE
    page_blk = (None, None, SB_HEADS, HEAD_DIM, PAGE_SIZE)

    def page_spec(p):
        return pl.BlockSpec(page_blk, lambda b, s, pt: (layer, pt[b, (n_steps - 1 - s) * pages + p], 0, 0, 0))

    per_b = pl.BlockSpec((None, SB_HEADS, HEAD_DIM, 1), lambda b, s, pt: (b, 0, 0, 0))
    grid_spec = pltpu.PrefetchScalarGridSpec(
        num_scalar_prefetch=1,
        grid=(nb, n_steps),
        in_specs=[per_b, per_b, per_b,
                  pl.BlockSpec((SB_HEADS, 1), lambda b, s, pt: (0, 0)),
                  pl.BlockSpec((PAGE_SIZE, PAGE_SIZE), lambda b, s, pt: (0, 0))]
                 + [page_spec(p) for p in range(pages)] * 2,
        out_specs=per_b,
        scratch_shapes=_sample_scratch(),
    )
    return pl.pallas_call(
        functools.partial(_sb_sample_kernel, pages=pages, past=past),
        grid_spec=grid_spec,
        out_shape=jax.ShapeDtypeStruct((nb, SB_HEADS, HEAD_DIM, 1), F32),
        compiler_params=_params("parallel", "arbitrary"),
        name="sb_sample",
    )(page_table, q, k_own, v_own, bias_col, u, *([cache_kt] * pages), *([cache_vt] * pages))


def _gla_sample_kernel(a_ref, s_ref, k_ref, v_ref, q_ref, sg_ref, gn_ref, s_out, og_out):
    s_new = a_ref[...] * s_ref[...] + k_ref[...] * v_ref[...]
    s_out[...] = s_new
    o = jnp.sum(q_ref[...] * s_new, axis=2)
    og_out[...] = _rms(o, gn_ref[...]) * sg_ref[...]


def _gla_sample(a, s0, gk, gv, gq, sg, gn):
    nb = s0.shape[0]
    col = lambda t: t.reshape(nb, GLA_HEADS, GLA_DK, 1)
    args = (col(a), s0, col(gk), gv.reshape(nb, GLA_HEADS, 1, GLA_DV), col(gq),
            sg.reshape(nb, GLA_HEADS, GLA_DV), gn.reshape(1, 1, GLA_DV))
    return pl.pallas_call(
        _gla_sample_kernel,
        out_shape=[jax.ShapeDtypeStruct(s0.shape, F32), jax.ShapeDtypeStruct((nb, GLA_HEADS, GLA_DV), F32)],
        name="gla_sample",
    )(*args)


def _tile(t, target):
    return target if t % target == 0 else t


def kernel(x_prompt, x_sample, cache_k, cache_v, state_gla, page_table, ffn1_norm_g, ffn1_w_up, ffn1_w_down, mix_norm_g, w_in, q_norm_g, k_norm_g, sb_logit_bias, gla_w_gate2, gla_b_gate, gla_out_norm_g, cm_ln_g, cm_ln_b, cm_w_spatial, cm_b_spatial, w_out, ffn2_norm_g, ffn2_w_up, ffn2_w_down):
    nb_p, L, d = x_prompt.shape
    nb_s, n_dec, _ = x_sample.shape
    depth = w_in.shape[0]
    assert n_dec == 1 and d == D_MODEL and L % CM_CHUNK == 0
    tp = nb_p * L
    tm = _tile(L, 512)
    tq = _tile(L, 256)
    tf = D_FF // 2
    pages = 8 if page_table.shape[1] % 8 == 0 else 1

    idx = jnp.arange(tm)
    same_chunk = (idx[:, None] // GLA_CHUNK) == (idx[None, :] // GLA_CHUNK)
    cum = (same_chunk & (idx[None, :] <= idx[:, None])).astype(BF16)
    tot = same_chunk.astype(BF16)
    hsum512 = ((jnp.arange(SB_WIDTH)[:, None] // HEAD_DIM) == (jnp.arange(SB_WIDTH)[None, :] // HEAD_DIM)).astype(BF16)
    hsum256 = hsum512[:GLA_WIDTH, :GLA_WIDTH]
    iq = jnp.arange(tq)
    u_later = (iq[:, None] > iq[None, :]).astype(BF16)
    ip = jnp.arange(PAGE_SIZE)
    u_page = (ip[:, None] > ip[None, :]).astype(F32)
    cache_kt = jnp.transpose(cache_k, (0, 1, 3, 4, 2))
    cache_vt = jnp.transpose(cache_v, (0, 1, 3, 4, 2))

    xp = x_prompt.reshape(tp, d)
    xs = x_sample.reshape(nb_s * n_dec, d)
    row = lambda v: v.reshape(1, -1).astype(F32)
    outs = {k: [] for k in ("kp", "vp", "sp", "ks", "vs", "ss", "cv")}
    for l in range(depth):
        w = w_in[l]
        w_r = jnp.concatenate([w[:, :C_GG], w[:, C_GG + GLA_RANK:], w[:, C_GG:C_GG + GLA_RANK],
                               jnp.zeros((d, N_IN_PAD - w.shape[1]), w.dtype)], axis=1).astype(BF16)
        wg2 = jnp.concatenate([gla_w_gate2[l], jnp.zeros((LANES - GLA_RANK, GLA_KEY_WIDTH), F32)], axis=0)
        mw = (row(mix_norm_g[l]), w_r, row(jnp.tile(q_norm_g[l], SB_HEADS)), row(jnp.tile(k_norm_g[l], SB_HEADS)),
              wg2, row(gla_b_gate[l]), row(cm_ln_g[l]), row(cm_ln_b[l]), hsum512)
        f1 = (row(ffn1_norm_g[l]), ffn1_w_up[l].astype(BF16), ffn1_w_down[l].astype(BF16))
        f2 = (row(ffn2_norm_g[l]), ffn2_w_up[l].astype(BF16), ffn2_w_down[l].astype(BF16))
        wo = w_out[l].astype(BF16)
        gn_row = row(jnp.tile(gla_out_norm_g[l], GLA_HEADS))
        bs_rows = jnp.repeat(cm_b_spatial[l].T, CM_GROUP_DIM, axis=1)

        xp = _ffn(xp, *f1, tm=tm, tf=tf)
        (q, kf, vf, kb, vb, qt, kt, kd, gv, eb, sg, ocm) = _mix_prompt(xp, mw, cum, tot, cm_w_spatial[l], bs_rows, tm=tm,
                                                                         n=nb_p, L=L)
        ts = xs.shape[0]
        xs = _ffn(xs, *f1, tm=ts, tf=tf)
        w00_row = row(jnp.repeat(cm_w_spatial[l][:, 0, 0], CM_GROUP_DIM))
        b0_row = row(jnp.repeat(cm_b_spatial[l][:, 0], CM_GROUP_DIM))
        (q_s, k_s, v_s, gq_s, gk_s, gv_s, a_s, sg_s, cv_s, ocm_s) = _mix_sample(xs, mw, w00_row, b0_row)
        col4 = lambda t: t.reshape(nb_s, SB_HEADS, HEAD_DIM, 1)
        sample = (page_table, col4(q_s), col4(k_s), col4(v_s), sb_logit_bias[l].reshape(SB_HEADS, 1), u_page,
                  cache_kt, cache_vt, l, pages)

        r3 = lambda t: t.reshape(nb_p, L, SB_WIDTH)
        if _sb_hosting_fits(nb_p, L, tq, page_table, pages):
            o_sb, o_sb_s = _sb_prompt(r3(q), r3(kb), r3(vb), sb_logit_bias[l], u_later, tq=tq, sample=sample)
        else:
            o_sb = _sb_prompt(r3(q), r3(kb), r3(vb), sb_logit_bias[l], u_later, tq=tq)
            o_sb_s = _sb_sample(*sample[:-2], layer=l, pages=pages)
        o_sb = o_sb.reshape(tp, SB_WIDTH)
        o_gla, st = _gla_prompt(qt, kt, kd, gv, eb, sg, gn_row, hsum256, n=nb_p, L=L, tc=tm)
        xp = _ffn(xp, *f2, tm=tm, tf=tf, merge=(o_sb, o_gla, ocm, wo))
        to_cache = lambda t: t.reshape(nb_p, SB_HEADS, HEAD_DIM, L).transpose(0, 3, 1, 2)
        outs["kp"].append(to_cache(kf))
        outs["vp"].append(to_cache(vf))
        st5 = st.reshape(nb_p, GLA_HEADS, GLA_DV, GLA_HEADS, GLA_DK)
        outs["sp"].append(jnp.stack([st5[:, h, :, h, :] for h in range(GLA_HEADS)], axis=1).transpose(0, 1, 3, 2))

        s_new, og = _gla_sample(a_s, state_gla[l], gk_s, gv_s, gq_s, sg_s, gla_out_norm_g[l])
        xs = _ffn(xs, *f2, tm=ts, tf=tf,
                  merge=(o_sb_s.reshape(ts, SB_WIDTH).astype(BF16), og.reshape(ts, GLA_WIDTH).astype(BF16), ocm_s, wo))
        outs["ks"].append(k_s.reshape(nb_s, n_dec, SB_HEADS, HEAD_DIM))
        outs["vs"].append(v_s.reshape(nb_s, n_dec, SB_HEADS, HEAD_DIM))
        outs["ss"].append(s_new)
        outs["cv"].append(cv_s.reshape(nb_s, n_dec, CM_GROUPS, CM_GROUP_DIM))

    st = lambda k: jnp.stack(outs[k])
    return (xp.reshape(nb_p, L, d), xs.reshape(nb_s, n_dec, d), st("kp"), st("vp"), st("sp"),
            st("ks"), st("vs"), st("ss"), st("cv"))
```

```python
import functools

import jax
import jax.numpy as jnp
from jax import lax
from jax.experimental import pallas as pl
from jax.experimental.pallas import tpu as pltpu

F32 = jnp.float32
BF16 = jnp.bfloat16

D_MODEL = 1024
HEAD_DIM = 64
SB_HEADS = 8
SB_WIDTH = SB_HEADS * HEAD_DIM
GLA_HEADS = 4
GLA_DV = 64
GLA_DK = 32
GLA_WIDTH = GLA_HEADS * GLA_DV
GLA_KEY_WIDTH = GLA_HEADS * GLA_DK
GLA_RANK = 16
GLA_TAU = 16.0
GLA_CHUNK = 64
CM_GROUPS = 4
CM_WIDTH = 256
CM_GROUP_DIM = 64
CM_CHUNK = 128
D_FF = 2816
EPS = 1e-6
PAGE_SIZE = 128

LOG2E = 1.4426950408889634
MASKED_LOG = -1e30
UNROLL = 4
SB_STEP_HEADS = 2
LANES = 128
VMEM_LIMIT = 56 * 1024 * 1024

C_SQ, C_SK, C_SV = 0, 512, 1024
C_GQ, C_GK, C_GV, C_GG = 1536, 1664, 1792, 2048
C_CU, C_CV, C_LR = 2304, 2560, 2816
N_IN_PAD = 2944

NT = (((1,), (1,)), ((), ()))
TN = (((0,), (0,)), ((), ()))


def _params(*sem):
    return pltpu.CompilerParams(dimension_semantics=sem, vmem_limit_bytes=VMEM_LIMIT)


def _const_spec(shape):
    n = len(shape)
    return pl.BlockSpec(shape, lambda *_: (0,) * n)


def _softplus(z):
    return jnp.maximum(z, 0.0) + jnp.log1p(jnp.exp(-jnp.abs(z)))


def _softplus2(x):
    neg_abs = pltpu.bitcast(pltpu.bitcast(x, jnp.uint32) | jnp.uint32(0x80000000), F32)
    return jnp.maximum(x, 0.0) + jnp.log2(1.0 + jnp.exp2(neg_abs))


def _split_bf16(x):
    hi = x.astype(BF16)
    lo = (x - hi.astype(F32)).astype(BF16)
    return hi, lo


def _dot(a, b):
    return jnp.dot(a, b, preferred_element_type=F32)


def _rms(x, g_row):
    return x * lax.rsqrt(jnp.mean(x * x, axis=-1, keepdims=True) + EPS) * g_row


def _ffn_kernel(*refs, nf, merged):
    if merged:
        x_ref, sb_ref, gla_ref, cm_ref, wo_ref, g_ref, wg_ref, wu_ref, wd_ref, o_ref, h_ref, acc_ref, x1_ref = refs
    else:
        x_ref, g_ref, wg_ref, wu_ref, wd_ref, o_ref, h_ref, acc_ref = refs
        x1_ref = x_ref
    f = pl.program_id(1)

    @pl.when(f == 0)
    def _():
        if merged:
            o = _dot(sb_ref[...], wo_ref[0:SB_WIDTH, :])
            o = o + _dot(gla_ref[...], wo_ref[SB_WIDTH:SB_WIDTH + GLA_WIDTH, :])
            o = o + _dot(cm_ref[...], wo_ref[SB_WIDTH + GLA_WIDTH:, :])
            x1_ref[...] = x_ref[...] + o
        h_ref[...] = _rms(x1_ref[...], g_ref[...]).astype(BF16)
        acc_ref[...] = jnp.zeros_like(acc_ref)

    h = h_ref[...]
    gate = _dot(h, wg_ref[...])
    up = _dot(h, wu_ref[...])
    act = (gate * jax.nn.sigmoid(gate) * up).astype(BF16)
    acc_ref[...] += _dot(act, wd_ref[...])

    @pl.when(f == nf - 1)
    def _():
        o_ref[...] = x1_ref[...] + 0.5 * acc_ref[...]


def _ffn(x, g_row, w_up, w_down, *, tm, tf, merge=None):
    t, d = x.shape
    nf = D_FF // tf
    row = lambda w: pl.BlockSpec((tm, w), lambda i, f: (i, 0))
    merge_specs, merge_args, scratch = [], (), []
    if merge is not None:
        merge_specs = [row(SB_WIDTH), row(GLA_WIDTH), row(CM_WIDTH), _const_spec((d, d))]
        merge_args = tuple(merge)
        scratch = [pltpu.VMEM((tm, d), F32)]
    return pl.pallas_call(
        functools.partial(_ffn_kernel, nf=nf, merged=merge is not None),
        grid=(t // tm, nf),
        in_specs=[row(d)] + merge_specs + [
            _const_spec((1, d)),
            pl.BlockSpec((d, tf), lambda i, f: (0, f)),
            pl.BlockSpec((d, tf), lambda i, f: (0, f + nf)),
            pl.BlockSpec((tf, d), lambda i, f: (f, 0)),
        ],
        out_specs=row(d),
        out_shape=jax.ShapeDtypeStruct((t, d), F32),
        scratch_shapes=[pltpu.VMEM((tm, d), BF16), pltpu.VMEM((tm, d), F32)] + scratch,
        compiler_params=_params("parallel", "arbitrary"),
        name="half_ffn_merged" if merge is not None else "half_ffn",
    )(x, *merge_args, g_row, w_up, w_up, w_down)


def _mix_common(x_ref, g_ref, w_ref, qg_ref, kg_ref, wg2_ref, bg_ref, lng_ref, lnb_ref, hsum_ref):
    h = _rms(x_ref[...], g_ref[...]).astype(BF16)

    def proj(lo, width):
        return _dot(h, w_ref[:, lo:lo + width])

    def head_norm(t, gain_row):
        ms = _dot((t * t).astype(BF16), hsum_ref[...]) * (1.0 / HEAD_DIM)
        return t * lax.rsqrt(ms + EPS) * gain_row

    out = {}
    out["q"] = head_norm(proj(C_SQ, SB_WIDTH), qg_ref[...]) * (HEAD_DIM ** -0.5)
    out["k"] = head_norm(proj(C_SK, SB_WIDTH), kg_ref[...])
    out["v"] = proj(C_SV, SB_WIDTH)
    out["gq"] = proj(C_GQ, GLA_KEY_WIDTH) * (GLA_DK ** -0.5)
    out["gk"] = proj(C_GK, GLA_KEY_WIDTH)
    out["gv"] = proj(C_GV, GLA_WIDTH)
    gg = proj(C_GG, GLA_WIDTH)
    out["sg"] = gg * jax.nn.sigmoid(gg)
    lr_hi, lr_lo = _split_bf16(proj(C_LR, LANES))
    w_hi, w_lo = _split_bf16(wg2_ref[...])
    pre = _dot(lr_hi, w_hi) + _dot(lr_lo, w_hi) + _dot(lr_hi, w_lo) + bg_ref[...]
    out["la"] = -_softplus(-pre) * (1.0 / GLA_TAU)
    out["cu"] = jax.nn.gelu(proj(C_CU, CM_WIDTH))
    c = jax.nn.gelu(proj(C_CV, CM_WIDTH))
    c = c - jnp.mean(c, axis=-1, keepdims=True)
    out["cv"] = c * lax.rsqrt(jnp.mean(c * c, axis=-1, keepdims=True) + EPS) * lng_ref[...] + lnb_ref[...]
    return out


def _mix_prompt_kernel(x_ref, g_ref, w_ref, qg_ref, kg_ref, wg2_ref, bg_ref, lng_ref, lnb_ref, hsum_ref,
                       cum_ref, tot_ref, ws_ref, bs_ref, *refs, tm, n_prev):
    prev_k, prev_v = refs[:2] if n_prev else (None, None)
    q_o, kf_o, vf_o, kb_o, vb_o, qt_o, kt_o, kd_o, gv_o, eb_o, sg_o, ocm_o = refs[2 if n_prev else 0:]
    m = _mix_common(x_ref, g_ref, w_ref, qg_ref, kg_ref, wg2_ref, bg_ref, lng_ref, lnb_ref, hsum_ref)
    q_o[...] = (m["q"] * LOG2E).astype(BF16)
    if n_prev:
        kf_o[0:n_prev] = prev_k[...]
        vf_o[0:n_prev] = prev_v[...]
    kf_o[n_prev] = m["k"].T
    vf_o[n_prev] = m["v"].T
    kb_o[...] = m["k"].astype(BF16)
    vb_o[...] = m["v"].astype(BF16)
    gv_o[...] = m["gv"].astype(BF16)
    sg_o[...] = m["sg"]
    la_hi, la_lo = _split_bf16(m["la"])
    b = _dot(cum_ref[...], la_hi) + _dot(cum_ref[...], la_lo)
    b_last = _dot(tot_ref[...], la_hi) + _dot(tot_ref[...], la_lo)
    eb = jnp.exp(b)
    eb_o[...] = eb
    qt_o[...] = (m["gq"] * eb).astype(BF16)
    kt_o[...] = (m["gk"] * jnp.exp(-b)).astype(BF16)
    kd_o[...] = (m["gk"] * jnp.exp(b_last - b)).astype(BF16)
    row = lax.broadcasted_iota(jnp.int32, (CM_CHUNK, CM_CHUNK), 0)
    col = lax.broadcasted_iota(jnp.int32, (CM_CHUNK, CM_CHUNK), 1)
    group = lax.broadcasted_iota(jnp.int32, (1, CM_WIDTH), 1) // CM_GROUP_DIM
    w_tril = [jnp.where(row >= col, ws_ref[g], 0.0).astype(BF16) for g in range(CM_GROUPS)]
    cv = m["cv"].astype(BF16)
    for c in range(tm // CM_CHUNK):
        rows = slice(c * CM_CHUNK, (c + 1) * CM_CHUNK)
        mixed = bs_ref[...]
        for g in range(CM_GROUPS):
            mixed = mixed + _dot(w_tril[g], jnp.where(group == g, cv[rows], jnp.zeros_like(cv[rows])))
        ocm_o[rows, :] = (m["cu"][rows] * mixed).astype(BF16)


def _mix_sample_kernel(x_ref, g_ref, w_ref, qg_ref, kg_ref, wg2_ref, bg_ref, lng_ref, lnb_ref, hsum_ref,
                       w00_ref, b0_ref,
                       q_o, k_o, v_o, gq_o, gk_o, gv_o, a_o, sg_o, cv_o, ocm_o):
    m = _mix_common(x_ref, g_ref, w_ref, qg_ref, kg_ref, wg2_ref, bg_ref, lng_ref, lnb_ref, hsum_ref)
    q_o[...] = m["q"]
    k_o[...] = m["k"]
    v_o[...] = m["v"]
    gq_o[...] = m["gq"]
    gk_o[...] = m["gk"]
    gv_o[...] = m["gv"]
    a_o[...] = jnp.exp(m["la"])
    sg_o[...] = m["sg"]
    cv_o[...] = m["cv"]
    ocm_o[...] = (m["cu"] * (w00_ref[...] * m["cv"] + b0_ref[...])).astype(BF16)


def _mix_weight_specs():
    return [
        _const_spec((1, D_MODEL)),
        _const_spec((D_MODEL, N_IN_PAD)),
        _const_spec((1, SB_WIDTH)),
        _const_spec((1, SB_WIDTH)),
        _const_spec((LANES, GLA_KEY_WIDTH)),
        _const_spec((1, GLA_KEY_WIDTH)),
        _const_spec((1, CM_WIDTH)),
        _const_spec((1, CM_WIDTH)),
        _const_spec((SB_WIDTH, SB_WIDTH)),
    ]


def _mix_prompt(x, mw, cum, tot, w_s, bs_rows, *, tm, n, L, prev_kv=()):
    t = x.shape[0]
    nt = L // tm
    n_prev = prev_kv[0].shape[0] if prev_kv else 0
    kv_block = lambda layers: pl.BlockSpec((layers, None, SB_WIDTH, tm), lambda i: (0, i // nt, 0, i % nt))
    kv_spec = kv_block(n_prev + 1)
    kv_sds = jax.ShapeDtypeStruct((n_prev + 1, n, SB_WIDTH, L), F32)
    row = lambda w: pl.BlockSpec((tm, w), lambda i: (i, 0))
    sds = lambda w, dt: jax.ShapeDtypeStruct((t, w), dt)
    widths = [(SB_WIDTH, BF16), (SB_WIDTH, F32), (SB_WIDTH, F32), (SB_WIDTH, BF16), (SB_WIDTH, BF16),
              (GLA_KEY_WIDTH, BF16), (GLA_KEY_WIDTH, BF16), (GLA_KEY_WIDTH, BF16), (GLA_WIDTH, BF16),
              (GLA_KEY_WIDTH, F32), (GLA_WIDTH, F32), (CM_WIDTH, BF16)]
    return pl.pallas_call(
        functools.partial(_mix_prompt_kernel, tm=tm, n_prev=n_prev),
        grid=(t // tm,),
        in_specs=[row(D_MODEL)] + _mix_weight_specs() + [
            _const_spec((tm, tm)), _const_spec((tm, tm)),
            _const_spec((CM_GROUPS, CM_CHUNK, CM_CHUNK)), _const_spec((CM_CHUNK, CM_WIDTH))]
                 + [kv_block(n_prev)] * len(prev_kv),
        out_specs=[kv_spec if i in (1, 2) else row(w) for i, (w, _) in enumerate(widths)],
        out_shape=[kv_sds if i in (1, 2) else sds(w, dt) for i, (w, dt) in enumerate(widths)],
        compiler_params=_params("parallel"),
        name="mix_prompt",
    )(x, *mw, cum, tot, w_s, bs_rows, *prev_kv)


def _mix_sample(x, mw, w00_row, b0_row):
    t = x.shape[0]
    full = lambda w: _const_spec((t, w))
    widths = [(SB_WIDTH, F32), (SB_WIDTH, F32), (SB_WIDTH, F32), (GLA_KEY_WIDTH, F32), (GLA_KEY_WIDTH, F32),
              (GLA_WIDTH, F32), (GLA_KEY_WIDTH, F32), (GLA_WIDTH, F32), (CM_WIDTH, F32), (CM_WIDTH, BF16)]
    return pl.pallas_call(
        _mix_sample_kernel,
        grid=(1,),
        in_specs=[full(D_MODEL)] + _mix_weight_specs() + [_const_spec((1, CM_WIDTH)), _const_spec((1, CM_WIDTH))],
        out_specs=[full(w) for w, _ in widths],
        out_shape=[jax.ShapeDtypeStruct((t, w), dt) for w, dt in widths],
        compiler_params=_params("arbitrary"),
        name="mix_sample",
    )(x, *mw, w00_row, b0_row)


def _sb_prompt_kernel(*refs, tq, hosted):
    hp = pl.program_id(1)
    qi = pl.program_id(2)
    if hosted is None:
        bias_ref, q_ref, k_ref, v_ref, u_ref, o_ref, acc_ref, z_ref, w_ref, c_ref, r_ref = refs
    else:
        pages, steps_per_row, past = hosted
        bias_ref, q_ref, k_ref, v_ref, u_ref = refs[1:6]
        s_in = refs[6:6 + N_SAMPLE_IN]
        pages_at = 6 + N_SAMPLE_IN
        s_k, s_v = refs[pages_at:pages_at + pages], refs[pages_at + pages:pages_at + 2 * pages]
        o_ref, s_o, acc_ref, z_ref, w_ref, c_ref, r_ref, s_qb, s_acc, s_carry = refs[pages_at + 2 * pages:]
        s_step = qi % steps_per_row
        _sample_sweep_step(s_step, s_step == steps_per_row - 1, *s_in, s_k, s_v, s_o, s_qb, s_acc, s_carry,
                           past=past)
    nh = q_ref.shape[-1] // HEAD_DIM
    lane_head = lax.broadcasted_iota(jnp.int32, (1, LANES), 1) // HEAD_DIM
    pair_lanes = [slice((hh // 2) * LANES, (hh // 2 + 1) * LANES) for hh in range(nh)]
    lane = lax.broadcasted_iota(jnp.int32, (1, LANES), 1)
    q_heads, k_fill = [], []
    for hh in range(nh):
        free = (1 - hh % 2) * HEAD_DIM
        rest = jnp.full((1, LANES), bias_ref[nh * hp + hh] * LOG2E, F32)
        parts = jnp.zeros((1, LANES), F32)
        for i in range(3):
            part = rest.astype(BF16).astype(F32)
            parts = jnp.where(lane == free + i, part, parts)
            rest = rest - part
        fill = jnp.broadcast_to(parts, (tq, LANES)).astype(BF16)
        q_heads.append(jnp.where(lane_head == hh % 2, q_ref[:, pair_lanes[hh]], fill))
        ones = jnp.where(jnp.logical_and(lane >= free, lane < free + 3), 1.0, 0.0)
        k_fill.append(jnp.broadcast_to(ones, (tq, LANES)).astype(BF16))
    row = lax.broadcasted_iota(jnp.int32, (tq, tq), 0)
    col = lax.broadcasted_iota(jnp.int32, (tq, tq), 1)
    heads = [slice(hh * tq, (hh + 1) * tq) for hh in range(nh)]

    def scores(j):
        rows = pl.ds(pl.multiple_of(j * tq, tq), tq)
        return jnp.concatenate(
            [lax.dot_general(q_heads[hh], jnp.where(lane_head == hh % 2, k_ref[rows, pair_lanes[hh]], k_fill[hh]),
                             NT, preferred_element_type=F32) for hh in range(nh)], axis=0)

    def log_terms(p, diagonal):
        for hh in range(nh):
            z = z_ref[p, heads[hh], :]
            sp = _softplus2(z)
            if diagonal:
                sp = jnp.where(col < row, sp, 0.0)
            w = z - sp - _dot(sp.astype(BF16), u_ref[...])
            if diagonal:
                w = jnp.where(col < row, w, MASKED_LOG)
            w_ref[p, heads[hh], :] = w
            r_ref[heads[hh], :] = jnp.broadcast_to(jnp.sum(sp, axis=1, keepdims=True), (tq, LANES))

    def attend(j, p):
        rows = pl.ds(pl.multiple_of(j * tq, tq), tq)
        a = []
        for hh in range(nh):
            c = c_ref[p, heads[hh], :]
            a.append(jnp.exp2(w_ref[p, heads[hh], :] + jnp.concatenate([c] * (tq // LANES), axis=1)).astype(BF16))
        for hh in range(0, nh, 2):
            o2 = _dot(jnp.concatenate(a[hh:hh + 2], axis=0), v_ref[rows, pair_lanes[hh]])
            acc_ref[:, pair_lanes[hh]] += jnp.where(lane_head == 0, o2[:tq], o2[tq:])

    def step(n, p, with_attend=True):
        if with_attend:
            attend(qi - n + 2, p)
        c_ref[p] = c_ref[1 - p] - r_ref[...]
        z_ref[1 - p] = scores(jnp.maximum(qi - n - 1, 0))
        log_terms(p, False)

    acc_ref[...] = jnp.zeros_like(acc_ref)
    c_ref[0] = jnp.zeros(c_ref.shape[1:], F32)
    z_ref[0] = scores(qi)
    log_terms(0, True)
    z_ref[1] = scores(jnp.maximum(qi - 1, 0))

    @pl.when(qi >= 1)
    def _():
        step(1, 1, with_attend=False)

    left = jnp.maximum(qi - 1, 0)

    def body(i, _):
        for k in range(UNROLL):
            step(2 + UNROLL * i + k, k % 2)
        return 0

    lax.fori_loop(0, left // UNROLL, body, 0)
    nxt = 2 + (left // UNROLL) * UNROLL
    rem = left % UNROLL

    @pl.when(rem >= 2)
    def _():
        step(nxt, 0)
        step(nxt + 1, 1)

    @pl.when(rem % 2 == 1)
    def _():
        step(qi, 0)

    @pl.when(qi >= 1)
    def _():
        attend(1, (qi - 1) % 2)

    attend(0, qi % 2)
    o_ref[...] = acc_ref[...].astype(BF16)


def _sb_hosted_pages(n, L, tq, page_table):
    nb, n_pages = page_table.shape
    steps = n * (SB_HEADS // SB_STEP_HEADS) * (L // tq)
    if steps % nb:
        return None
    steps_per_row = steps // nb
    if n_pages % steps_per_row or (L // tq) % steps_per_row:
        return None
    return n_pages // steps_per_row


def _sb_prompt(q, k, v, bias, u, *, tq, sample=None):
    n, L, _ = q.shape
    nh, nq = SB_HEADS // SB_STEP_HEADS, L // tq
    width, rows_z = SB_STEP_HEADS * HEAD_DIM, SB_STEP_HEADS * tq
    blk = lambda rows: (None, rows, width)
    in_specs = [
        pl.BlockSpec(memory_space=pltpu.SMEM),
        pl.BlockSpec(blk(tq), lambda b, h, i, *_: (b, i, h)),
        pl.BlockSpec(blk(L), lambda b, h, i, *_: (b, 0, h)),
        pl.BlockSpec(blk(L), lambda b, h, i, *_: (b, 0, h)),
        pl.BlockSpec((tq, tq), lambda b, h, i, *_: (0, 0)),
    ]
    out_specs = pl.BlockSpec(blk(tq), lambda b, h, i, *_: (b, i, h))
    out_shape = jax.ShapeDtypeStruct((n, L, SB_WIDTH), BF16)
    scratch = [pltpu.VMEM((tq, width), F32), pltpu.VMEM((2, rows_z, tq), F32), pltpu.VMEM((2, rows_z, tq), F32),
               pltpu.VMEM((2, rows_z, LANES), F32), pltpu.VMEM((rows_z, LANES), F32)]
    if sample is None:
        return pl.pallas_call(
            functools.partial(_sb_prompt_kernel, tq=tq, hosted=None),
            grid=(n, nh, nq), in_specs=in_specs, out_specs=out_specs, out_shape=out_shape, scratch_shapes=scratch,
            compiler_params=_params("parallel", "parallel", "arbitrary"),
            name="sb_prompt",
        )(bias, q, k, v, u)

    page_table, s_q, s_k, s_v, s_bias, s_u, cache_kt, cache_vt, layer, pages = sample
    nb, n_pages = page_table.shape
    spr = n_pages // pages
    row_of = lambda b, h, i: (b * nh + h) * (nq // spr) + i // spr

    def page_spec(p):
        return pl.BlockSpec(
            (None, None, SB_HEADS, HEAD_DIM, PAGE_SIZE),
            lambda b, h, i, pt: (layer, pt[row_of(b, h, i), (spr - 1 - i % spr) * pages + p], 0, 0, 0))

    per_row = pl.BlockSpec((None, SB_HEADS, HEAD_DIM, 1), lambda b, h, i, pt: (row_of(b, h, i), 0, 0, 0))
    grid_spec = pltpu.PrefetchScalarGridSpec(
        num_scalar_prefetch=1,
        grid=(n, nh, nq),
        in_specs=in_specs + [per_row, per_row, per_row,
                             pl.BlockSpec((SB_HEADS, 1), lambda b, h, i, pt: (0, 0)),
                             pl.BlockSpec((PAGE_SIZE, PAGE_SIZE), lambda b, h, i, pt: (0, 0))]
                 + [page_spec(p) for p in range(pages)] * 2,
        out_specs=[out_specs, per_row],
        scratch_shapes=scratch + _sample_scratch(),
    )
    return pl.pallas_call(
        functools.partial(_sb_prompt_kernel, tq=tq, hosted=(pages, spr, n_pages * PAGE_SIZE)),
        grid_spec=grid_spec,
        out_shape=[out_shape, jax.ShapeDtypeStruct((nb, SB_HEADS, HEAD_DIM, 1), F32)],
        compiler_params=_params("arbitrary", "arbitrary", "arbitrary"),
        name="sb_prompt_hosting_sample",
    )(page_table, bias, q, k, v, u, s_q, s_k, s_v, s_bias, s_u, *([cache_kt] * pages), *([cache_vt] * pages))


def _gla_prompt_kernel(qt_ref, kt_ref, kd_ref, v_ref, eb_ref, sg_ref, gn_ref, hsum_ref,
                       og_ref, st_ref, s_ref, o_ref, *, tc):
    @pl.when(pl.program_id(1) == 0)
    def _():
        s_ref[...] = jnp.zeros_like(s_ref)

    ck = GLA_CHUNK
    key_head = lax.broadcasted_iota(jnp.int32, (1, GLA_KEY_WIDTH), 1) // GLA_DK
    val_head = lax.broadcasted_iota(jnp.int32, (1, GLA_WIDTH), 1) // GLA_DV
    state_mask = (lax.broadcasted_iota(jnp.int32, (GLA_WIDTH, GLA_KEY_WIDTH), 0) // GLA_DV
                  == lax.broadcasted_iota(jnp.int32, (GLA_WIDTH, GLA_KEY_WIDTH), 1) // GLA_DK)
    causal = (lax.broadcasted_iota(jnp.int32, (ck, ck), 0) >= lax.broadcasted_iota(jnp.int32, (ck, ck), 1))
    for c in range(tc // ck):
        rows = slice(c * ck, (c + 1) * ck)
        qt, kt, kd, v = qt_ref[rows, :], kt_ref[rows, :], kd_ref[rows, :], v_ref[rows, :]
        att = []
        v_bd = []
        for hh in range(GLA_HEADS):
            qh = jnp.where(key_head == hh, qt, jnp.zeros_like(qt))
            s = lax.dot_general(qh, kt, NT, preferred_element_type=F32)
            att.append(jnp.where(causal, s, 0.0).astype(BF16))
            v_bd.append(jnp.where(val_head == hh, v, jnp.zeros_like(v)))
        st = s_ref[...]
        o = _dot(jnp.concatenate(att, axis=1), jnp.concatenate(v_bd, axis=0))
        o = o + lax.dot_general(qt, st.astype(BF16), NT, preferred_element_type=F32)
        o_ref[rows, :] = o
        ds = lax.dot_general(v, kd, TN, preferred_element_type=F32)
        decay = eb_ref[c * ck + ck - 1:c * ck + ck, :]
        s_ref[...] = decay * st + jnp.where(state_mask, ds, 0.0)
    o = o_ref[...]
    ms = _dot((o * o).astype(BF16), hsum_ref[...]) * (1.0 / GLA_DV)
    og_ref[...] = (o * lax.rsqrt(ms + EPS) * gn_ref[...] * sg_ref[...]).astype(BF16)
    st_ref[...] = s_ref[...]


def _gla_prompt(qt, kt, kd, gv, eb, sg, gn_row, hsum, *, n, L, tc):
    t = n * L
    nc = L // tc
    row = lambda w: pl.BlockSpec((tc, w), lambda b, i: (b * nc + i, 0))
    return pl.pallas_call(
        functools.partial(_gla_prompt_kernel, tc=tc),
        grid=(n, nc),
        in_specs=[row(GLA_KEY_WIDTH), row(GLA_KEY_WIDTH), row(GLA_KEY_WIDTH), row(GLA_WIDTH),
                  row(GLA_KEY_WIDTH), row(GLA_WIDTH), _const_spec((1, GLA_WIDTH)),
                  _const_spec((GLA_WIDTH, GLA_WIDTH))],
        out_specs=[row(GLA_WIDTH), pl.BlockSpec((None, GLA_WIDTH, GLA_KEY_WIDTH), lambda b, i: (b, 0, 0))],
        out_shape=[jax.ShapeDtypeStruct((t, GLA_WIDTH), BF16),
                   jax.ShapeDtypeStruct((n, GLA_WIDTH, GLA_KEY_WIDTH), F32)],
        scratch_shapes=[pltpu.VMEM((GLA_WIDTH, GLA_KEY_WIDTH), F32), pltpu.VMEM((tc, GLA_WIDTH), F32)],
        compiler_params=_params("parallel", "arbitrary"),
        name="gla_prompt",
    )(qt, kt, kd, gv, eb, sg, gn_row, hsum)


def _sample_sweep_step(s, is_last, q_ref, kown_ref, vown_ref, bias_ref, u_ref, k_refs, v_refs, o_ref,
                       qb_ref, acc_ref, carry_ref, *, past):
    pages = len(k_refs)
    bias = bias_ref[...]

    @pl.when(s == 0)
    def _():
        q = q_ref[...]
        qb_ref[...] = jnp.broadcast_to(q, qb_ref.shape)
        key_pos = past + 0 * lax.broadcasted_iota(jnp.int32, (SB_HEADS, 1), 0)
        valid = key_pos < past
        z = jnp.sum(q * kown_ref[...], axis=1) + bias
        sp = jnp.where(valid, _softplus(z), 0.0)
        a = jnp.where(valid, jnp.exp(z - sp), 0.0)
        lane = lax.broadcasted_iota(jnp.int32, acc_ref.shape, 2)
        own = jnp.broadcast_to(a[:, :, None] * vown_ref[...], acc_ref.shape)
        acc_ref[...] = jnp.where(lane == 0, own, 0.0)
        carry_ref[...] = -sp

    order = list(reversed(range(pages)))
    qb = qb_ref[...]
    z = jnp.concatenate([jnp.sum(k_refs[p][...] * qb, axis=1) for p in order], axis=0)
    z = z + jnp.concatenate([bias] * pages, axis=0)
    sp = _softplus(z)
    hi = sp.astype(BF16).astype(F32)
    lo = (sp - hi).astype(BF16).astype(F32)
    later = _dot(hi, u_ref[...]) + _dot(lo, u_ref[...])
    totals = jnp.sum(sp, axis=1, keepdims=True)
    carries = [carry_ref[...]]
    for i in range(pages):
        carries.append(carries[-1] - totals[i * SB_HEADS:(i + 1) * SB_HEADS])
    carry_ref[...] = carries[-1]
    a = jnp.exp(z - sp - later + jnp.concatenate(carries[:-1], axis=0))
    for h in range(SB_HEADS):
        acc = acc_ref[h]
        for i, p in enumerate(order):
            acc = acc + a[i * SB_HEADS + h:i * SB_HEADS + h + 1, :] * v_refs[p][h]
        acc_ref[h] = acc

    @pl.when(is_last)
    def _():
        o_ref[...] = jnp.sum(acc_ref[...], axis=2, keepdims=True)


N_SAMPLE_IN = 5


def _sample_scratch():
    return [pltpu.VMEM((SB_HEADS, HEAD_DIM, PAGE_SIZE), F32), pltpu.VMEM((SB_HEADS, HEAD_DIM, PAGE_SIZE), F32),
            pltpu.VMEM((SB_HEADS, 1), F32)]


def _sb_sample_kernel(pt_ref, *refs, pages, past):
    ins, refs = refs[:N_SAMPLE_IN], refs[N_SAMPLE_IN:]
    k_refs, v_refs = refs[:pages], refs[pages:2 * pages]
    o_ref, qb_ref, acc_ref, carry_ref = refs[2 * pages:]
    s = pl.program_id(1)
    _sample_sweep_step(s, s == pl.num_programs(1) - 1, *ins, k_refs, v_refs, o_ref, qb_ref, acc_ref, carry_ref,
                       past=past)


def _sb_sample(page_table, q, k_own, v_own, bias_col, u, cache_kt, cache_vt, *, layer, pages):
    nb, n_pages = page_table.shape
    n_steps = n_pages // pages
    past = n_pages * PAGE_SIZE
    page_blk = (None, None, SB_HEADS, HEAD_DIM, PAGE_SIZE)

    def page_spec(p):
        return pl.BlockSpec(page_blk, lambda b, s, pt: (layer, pt[b, (n_steps - 1 - s) * pages + p], 0, 0, 0))

    per_b = pl.BlockSpec((None, SB_HEADS, HEAD_DIM, 1), lambda b, s, pt: (b, 0, 0, 0))
    grid_spec = pltpu.PrefetchScalarGridSpec(
        num_scalar_prefetch=1,
        grid=(nb, n_steps),
        in_specs=[per_b, per_b, per_b,
                  pl.BlockSpec((SB_HEADS, 1), lambda b, s, pt: (0, 0)),
                  pl.BlockSpec((PAGE_SIZE, PAGE_SIZE), lambda b, s, pt: (0, 0))]
                 + [page_spec(p) for p in range(pages)] * 2,
        out_specs=per_b,
        scratch_shapes=_sample_scratch(),
    )
    return pl.pallas_call(
        functools.partial(_sb_sample_kernel, pages=pages, past=past),
        grid_spec=grid_spec,
        out_shape=jax.ShapeDtypeStruct((nb, SB_HEADS, HEAD_DIM, 1), F32),
        compiler_params=_params("parallel", "arbitrary"),
        name="sb_sample",
    )(page_table, q, k_own, v_own, bias_col, u, *([cache_kt] * pages), *([cache_vt] * pages))


def _gla_sample_kernel(a_ref, s_ref, k_ref, v_ref, q_ref, sg_ref, gn_ref, s_out, og_out):
    s_new = a_ref[...] * s_ref[...] + k_ref[...] * v_ref[...]
    s_out[...] = s_new
    o = jnp.sum(q_ref[...] * s_new, axis=2)
    og_out[...] = _rms(o, gn_ref[...]) * sg_ref[...]


def _gla_sample(a, s0, gk, gv, gq, sg, gn):
    nb = s0.shape[0]
    col = lambda t: t.reshape(nb, GLA_HEADS, GLA_DK, 1)
    args = (col(a), s0, col(gk), gv.reshape(nb, GLA_HEADS, 1, GLA_DV), col(gq),
            sg.reshape(nb, GLA_HEADS, GLA_DV), gn.reshape(1, 1, GLA_DV))
    return pl.pallas_call(
        _gla_sample_kernel,
        out_shape=[jax.ShapeDtypeStruct(s0.shape, F32), jax.ShapeDtypeStruct((nb, GLA_HEADS, GLA_DV), F32)],
        name="gla_sample",
    )(*args)


def _tile(t, target):
    return target if t % target == 0 else t


def kernel(x_prompt, x_sample, cache_k, cache_v, state_gla, page_table, ffn1_norm_g, ffn1_w_up, ffn1_w_down, mix_norm_g, w_in, q_norm_g, k_norm_g, sb_logit_bias, gla_w_gate2, gla_b_gate, gla_out_norm_g, cm_ln_g, cm_ln_b, cm_w_spatial, cm_b_spatial, w_out, ffn2_norm_g, ffn2_w_up, ffn2_w_down):
    nb_p, L, d = x_prompt.shape
    nb_s, n_dec, _ = x_sample.shape
    depth = w_in.shape[0]
    assert n_dec == 1 and d == D_MODEL and L % CM_CHUNK == 0
    tp = nb_p * L
    tm = _tile(L, 512)
    tq = _tile(L, 256)
    tf = D_FF // 2
    hosted_pages = _sb_hosted_pages(nb_p, L, tq, page_table)
    pages = hosted_pages or (8 if page_table.shape[1] % 8 == 0 else 1)

    idx = jnp.arange(tm)
    same_chunk = (idx[:, None] // GLA_CHUNK) == (idx[None, :] // GLA_CHUNK)
    cum = (same_chunk & (idx[None, :] <= idx[:, None])).astype(BF16)
    tot = same_chunk.astype(BF16)
    hsum512 = ((jnp.arange(SB_WIDTH)[:, None] // HEAD_DIM) == (jnp.arange(SB_WIDTH)[None, :] // HEAD_DIM)).astype(BF16)
    hsum256 = hsum512[:GLA_WIDTH, :GLA_WIDTH]
    iq = jnp.arange(tq)
    u_later = (iq[:, None] > iq[None, :]).astype(BF16)
    ip = jnp.arange(PAGE_SIZE)
    u_page = (ip[:, None] > ip[None, :]).astype(F32)
    cache_kt = jnp.transpose(cache_k, (0, 1, 3, 4, 2))
    cache_vt = jnp.transpose(cache_v, (0, 1, 3, 4, 2))

    xp = x_prompt.reshape(tp, d)
    xs = x_sample.reshape(nb_s * n_dec, d)
    row = lambda v: v.reshape(1, -1).astype(F32)
    outs = {k: [] for k in ("sp", "ks", "vs", "ss", "cv")}
    kv_stack = ()
    for l in range(depth):
        w = w_in[l]
        w_r = jnp.concatenate([w[:, :C_GG], w[:, C_GG + GLA_RANK:], w[:, C_GG:C_GG + GLA_RANK],
                               jnp.zeros((d, N_IN_PAD - w.shape[1]), w.dtype)], axis=1).astype(BF16)
        wg2 = jnp.concatenate([gla_w_gate2[l], jnp.zeros((LANES - GLA_RANK, GLA_KEY_WIDTH), F32)], axis=0)
        mw = (row(mix_norm_g[l]), w_r, row(jnp.tile(q_norm_g[l], SB_HEADS)), row(jnp.tile(k_norm_g[l], SB_HEADS)),
              wg2, row(gla_b_gate[l]), row(cm_ln_g[l]), row(cm_ln_b[l]), hsum512)
        f1 = (row(ffn1_norm_g[l]), ffn1_w_up[l].astype(BF16), ffn1_w_down[l].astype(BF16))
        f2 = (row(ffn2_norm_g[l]), ffn2_w_up[l].astype(BF16), ffn2_w_down[l].astype(BF16))
        wo = w_out[l].astype(BF16)
        gn_row = row(jnp.tile(gla_out_norm_g[l], GLA_HEADS))
        bs_rows = jnp.repeat(cm_b_spatial[l].T, CM_GROUP_DIM, axis=1)

        xp = _ffn(xp, *f1, tm=tm, tf=tf)
        (q, kf, vf, kb, vb, qt, kt, kd, gv, eb, sg, ocm) = _mix_prompt(xp, mw, cum, tot, cm_w_spatial[l], bs_rows, tm=tm,
                                                                         n=nb_p, L=L, prev_kv=kv_stack)
        kv_stack = (kf, vf)
        ts = xs.shape[0]
        xs = _ffn(xs, *f1, tm=ts, tf=tf)
        w00_row = row(jnp.repeat(cm_w_spatial[l][:, 0, 0], CM_GROUP_DIM))
        b0_row = row(jnp.repeat(cm_b_spatial[l][:, 0], CM_GROUP_DIM))
        (q_s, k_s, v_s, gq_s, gk_s, gv_s, a_s, sg_s, cv_s, ocm_s) = _mix_sample(xs, mw, w00_row, b0_row)
        col4 = lambda t: t.reshape(nb_s, SB_HEADS, HEAD_DIM, 1)
        sample = (page_table, col4(q_s), col4(k_s), col4(v_s), sb_logit_bias[l].reshape(SB_HEADS, 1), u_page,
                  cache_kt, cache_vt, l, pages)

        r3 = lambda t: t.reshape(nb_p, L, SB_WIDTH)
        if hosted_pages:
            o_sb, o_sb_s = _sb_prompt(r3(q), r3(kb), r3(vb), sb_logit_bias[l], u_later, tq=tq, sample=sample)
        else:
            o_sb = _sb_prompt(r3(q), r3(kb), r3(vb), sb_logit_bias[l], u_later, tq=tq)
            o_sb_s = _sb_sample(*sample[:-2], layer=l, pages=pages)
        o_sb = o_sb.reshape(tp, SB_WIDTH)
        o_gla, st = _gla_prompt(qt, kt, kd, gv, eb, sg, gn_row, hsum256, n=nb_p, L=L, tc=tm)
        xp = _ffn(xp, *f2, tm=tm, tf=tf, merge=(o_sb, o_gla, ocm, wo))
        st5 = st.reshape(nb_p, GLA_HEADS, GLA_DV, GLA_HEADS, GLA_DK)
        outs["sp"].append(jnp.stack([st5[:, h, :, h, :] for h in range(GLA_HEADS)], axis=1).transpose(0, 1, 3, 2))

        s_new, og = _gla_sample(a_s, state_gla[l], gk_s, gv_s, gq_s, sg_s, gla_out_norm_g[l])
        xs = _ffn(xs, *f2, tm=ts, tf=tf,
                  merge=(o_sb_s.reshape(ts, SB_WIDTH).astype(BF16), og.reshape(ts, GLA_WIDTH).astype(BF16), ocm_s, wo))
        outs["ks"].append(k_s.reshape(nb_s, n_dec, SB_HEADS, HEAD_DIM))
        outs["vs"].append(v_s.reshape(nb_s, n_dec, SB_HEADS, HEAD_DIM))
        outs["ss"].append(s_new)
        outs["cv"].append(cv_s.reshape(nb_s, n_dec, CM_GROUPS, CM_GROUP_DIM))

    st = lambda k: jnp.stack(outs[k])
    to_cache = lambda t: t.reshape(depth, nb_p, SB_HEADS, HEAD_DIM, L).transpose(0, 1, 4, 2, 3)
    return (xp.reshape(nb_p, L, d), xs.reshape(nb_s, n_dec, d), to_cache(kv_stack[0]), to_cache(kv_stack[1]),
            st("sp"), st("ks"), st("vs"), st("ss"), st("cv"))
```

```python
import functools

import jax
import jax.numpy as jnp
from jax import lax
from jax.experimental import pallas as pl
from jax.experimental.pallas import tpu as pltpu

F32 = jnp.float32
BF16 = jnp.bfloat16

D_MODEL = 1024
HEAD_DIM = 64
SB_HEADS = 8
SB_WIDTH = SB_HEADS * HEAD_DIM
GLA_HEADS = 4
GLA_DV = 64
GLA_DK = 32
GLA_WIDTH = GLA_HEADS * GLA_DV
GLA_KEY_WIDTH = GLA_HEADS * GLA_DK
GLA_RANK = 16
GLA_TAU = 16.0
GLA_CHUNK = 64
CM_GROUPS = 4
CM_WIDTH = 256
CM_GROUP_DIM = 64
CM_CHUNK = 128
D_FF = 2816
EPS = 1e-6
PAGE_SIZE = 128

LOG2E = 1.4426950408889634
MASKED_LOG = -1e30
UNROLL = 4
SB_STEP_HEADS = 2
LANES = 128
VMEM_LIMIT = 56 * 1024 * 1024

C_SQ, C_SK, C_SV = 0, 512, 1024
C_GQ, C_GK, C_GV, C_GG = 1536, 1664, 1792, 2048
C_CU, C_CV, C_LR = 2304, 2560, 2816
N_IN_PAD = 2944

NT = (((1,), (1,)), ((), ()))
TN = (((0,), (0,)), ((), ()))


def _params(*sem):
    return pltpu.CompilerParams(dimension_semantics=sem, vmem_limit_bytes=VMEM_LIMIT)


def _const_spec(shape):
    n = len(shape)
    return pl.BlockSpec(shape, lambda *_: (0,) * n)


def _softplus(z):
    return jnp.maximum(z, 0.0) + jnp.log1p(jnp.exp(-jnp.abs(z)))


def _softplus2(x):
    neg_abs = pltpu.bitcast(pltpu.bitcast(x, jnp.uint32) | jnp.uint32(0x80000000), F32)
    return jnp.maximum(x, 0.0) + jnp.log2(1.0 + jnp.exp2(neg_abs))


def _split_bf16(x):
    hi = x.astype(BF16)
    lo = (x - hi.astype(F32)).astype(BF16)
    return hi, lo


def _dot(a, b):
    return jnp.dot(a, b, preferred_element_type=F32)


def _rms(x, g_row):
    return x * lax.rsqrt(jnp.mean(x * x, axis=-1, keepdims=True) + EPS) * g_row


def _ffn_kernel(*refs, nf, merged):
    if merged:
        x_ref, sb_ref, gla_ref, cm_ref, wo_ref, g_ref, wg_ref, wu_ref, wd_ref, o_ref, h_ref, acc_ref, x1_ref = refs
    else:
        x_ref, g_ref, wg_ref, wu_ref, wd_ref, o_ref, h_ref, acc_ref = refs
        x1_ref = x_ref
    f = pl.program_id(1)

    @pl.when(f == 0)
    def _():
        if merged:
            o = _dot(sb_ref[...], wo_ref[0:SB_WIDTH, :])
            o = o + _dot(gla_ref[...], wo_ref[SB_WIDTH:SB_WIDTH + GLA_WIDTH, :])
            o = o + _dot(cm_ref[...], wo_ref[SB_WIDTH + GLA_WIDTH:, :])
            x1_ref[...] = x_ref[...] + o
        h_ref[...] = _rms(x1_ref[...], g_ref[...]).astype(BF16)
        acc_ref[...] = jnp.zeros_like(acc_ref)

    h = h_ref[...]
    gate = _dot(h, wg_ref[...])
    up = _dot(h, wu_ref[...])
    act = (gate * jax.nn.sigmoid(gate) * up).astype(BF16)
    acc_ref[...] += _dot(act, wd_ref[...])

    @pl.when(f == nf - 1)
    def _():
        o_ref[...] = x1_ref[...] + 0.5 * acc_ref[...]


def _ffn(x, g_row, w_up, w_down, *, tm, tf, merge=None):
    t, d = x.shape
    nf = D_FF // tf
    row = lambda w: pl.BlockSpec((tm, w), lambda i, f: (i, 0))
    merge_specs, merge_args, scratch = [], (), []
    if merge is not None:
        merge_specs = [row(SB_WIDTH), row(GLA_WIDTH), row(CM_WIDTH), _const_spec((d, d))]
        merge_args = tuple(merge)
        scratch = [pltpu.VMEM((tm, d), F32)]
    return pl.pallas_call(
        functools.partial(_ffn_kernel, nf=nf, merged=merge is not None),
        grid=(t // tm, nf),
        in_specs=[row(d)] + merge_specs + [
            _const_spec((1, d)),
            pl.BlockSpec((d, tf), lambda i, f: (0, f)),
            pl.BlockSpec((d, tf), lambda i, f: (0, f + nf)),
            pl.BlockSpec((tf, d), lambda i, f: (f, 0)),
        ],
        out_specs=row(d),
        out_shape=jax.ShapeDtypeStruct((t, d), F32),
        scratch_shapes=[pltpu.VMEM((tm, d), BF16), pltpu.VMEM((tm, d), F32)] + scratch,
        compiler_params=_params("parallel", "arbitrary"),
        name="half_ffn_merged" if merge is not None else "half_ffn",
    )(x, *merge_args, g_row, w_up, w_up, w_down)


def _mix_common(x_ref, g_ref, w_ref, qg_ref, kg_ref, wg2_ref, bg_ref, lng_ref, lnb_ref, hsum_ref):
    h = _rms(x_ref[...], g_ref[...]).astype(BF16)

    def proj(lo, width):
        return _dot(h, w_ref[:, lo:lo + width])

    def head_norm(t, gain_row):
        ms = _dot((t * t).astype(BF16), hsum_ref[...]) * (1.0 / HEAD_DIM)
        return t * lax.rsqrt(ms + EPS) * gain_row

    out = {}
    out["q"] = head_norm(proj(C_SQ, SB_WIDTH), qg_ref[...]) * (HEAD_DIM ** -0.5)
    out["k"] = head_norm(proj(C_SK, SB_WIDTH), kg_ref[...])
    out["v"] = proj(C_SV, SB_WIDTH)
    out["gq"] = proj(C_GQ, GLA_KEY_WIDTH) * (GLA_DK ** -0.5)
    out["gk"] = proj(C_GK, GLA_KEY_WIDTH)
    out["gv"] = proj(C_GV, GLA_WIDTH)
    gg = proj(C_GG, GLA_WIDTH)
    out["sg"] = gg * jax.nn.sigmoid(gg)
    lr_hi, lr_lo = _split_bf16(proj(C_LR, LANES))
    w_hi, w_lo = _split_bf16(wg2_ref[...])
    pre = _dot(lr_hi, w_hi) + _dot(lr_lo, w_hi) + _dot(lr_hi, w_lo) + bg_ref[...]
    out["la"] = -_softplus(-pre) * (1.0 / GLA_TAU)
    out["cu"] = jax.nn.gelu(proj(C_CU, CM_WIDTH))
    c = jax.nn.gelu(proj(C_CV, CM_WIDTH))
    c = c - jnp.mean(c, axis=-1, keepdims=True)
    out["cv"] = c * lax.rsqrt(jnp.mean(c * c, axis=-1, keepdims=True) + EPS) * lng_ref[...] + lnb_ref[...]
    return out


def _mix_prompt_kernel(x_ref, g_ref, w_ref, qg_ref, kg_ref, wg2_ref, bg_ref, lng_ref, lnb_ref, hsum_ref,
                       cum_ref, tot_ref, ws_ref, bs_ref, *refs, tm, n_prev):
    prev_k, prev_v = refs[:2] if n_prev else (None, None)
    q_o, kf_o, vf_o, kb_o, vb_o, qt_o, kt_o, kd_o, gv_o, eb_o, sg_o, ocm_o = refs[2 if n_prev else 0:]
    m = _mix_common(x_ref, g_ref, w_ref, qg_ref, kg_ref, wg2_ref, bg_ref, lng_ref, lnb_ref, hsum_ref)
    q_o[...] = (m["q"] * LOG2E).astype(BF16)
    if n_prev:
        kf_o[0:n_prev] = prev_k[...]
        vf_o[0:n_prev] = prev_v[...]
    kf_o[n_prev] = m["k"].T
    vf_o[n_prev] = m["v"].T
    kb_o[...] = m["k"].astype(BF16)
    vb_o[...] = m["v"].astype(BF16)
    gv_o[...] = m["gv"].astype(BF16)
    sg_o[...] = m["sg"]
    la_hi, la_lo = _split_bf16(m["la"])
    b = _dot(cum_ref[...], la_hi) + _dot(cum_ref[...], la_lo)
    b_last = _dot(tot_ref[...], la_hi) + _dot(tot_ref[...], la_lo)
    eb = jnp.exp(b)
    eb_o[...] = eb
    qt_o[...] = (m["gq"] * eb).astype(BF16)
    kt_o[...] = (m["gk"] * jnp.exp(-b)).astype(BF16)
    kd_o[...] = (m["gk"] * jnp.exp(b_last - b)).astype(BF16)
    row = lax.broadcasted_iota(jnp.int32, (CM_CHUNK, CM_CHUNK), 0)
    col = lax.broadcasted_iota(jnp.int32, (CM_CHUNK, CM_CHUNK), 1)
    group = lax.broadcasted_iota(jnp.int32, (1, CM_WIDTH), 1) // CM_GROUP_DIM
    w_tril = [jnp.where(row >= col, ws_ref[g], 0.0).astype(BF16) for g in range(CM_GROUPS)]
    cv = m["cv"].astype(BF16)
    for c in range(tm // CM_CHUNK):
        rows = slice(c * CM_CHUNK, (c + 1) * CM_CHUNK)
        mixed = bs_ref[...]
        for g in range(CM_GROUPS):
            mixed = mixed + _dot(w_tril[g], jnp.where(group == g, cv[rows], jnp.zeros_like(cv[rows])))
        ocm_o[rows, :] = (m["cu"][rows] * mixed).astype(BF16)


def _mix_sample_kernel(x_ref, g_ref, w_ref, qg_ref, kg_ref, wg2_ref, bg_ref, lng_ref, lnb_ref, hsum_ref,
                       w00_ref, b0_ref,
                       q_o, k_o, v_o, gq_o, gk_o, gv_o, a_o, sg_o, cv_o, ocm_o):
    m = _mix_common(x_ref, g_ref, w_ref, qg_ref, kg_ref, wg2_ref, bg_ref, lng_ref, lnb_ref, hsum_ref)
    q_o[...] = m["q"]
    k_o[...] = m["k"]
    v_o[...] = m["v"]
    gq_o[...] = m["gq"]
    gk_o[...] = m["gk"]
    gv_o[...] = m["gv"]
    a_o[...] = jnp.exp(m["la"])
    sg_o[...] = m["sg"]
    cv_o[...] = m["cv"]
    ocm_o[...] = (m["cu"] * (w00_ref[...] * m["cv"] + b0_ref[...])).astype(BF16)


def _mix_weight_specs():
    return [
        _const_spec((1, D_MODEL)),
        _const_spec((D_MODEL, N_IN_PAD)),
        _const_spec((1, SB_WIDTH)),
        _const_spec((1, SB_WIDTH)),
        _const_spec((LANES, GLA_KEY_WIDTH)),
        _const_spec((1, GLA_KEY_WIDTH)),
        _const_spec((1, CM_WIDTH)),
        _const_spec((1, CM_WIDTH)),
        _const_spec((SB_WIDTH, SB_WIDTH)),
    ]


def _mix_prompt(x, mw, cum, tot, w_s, bs_rows, *, tm, n, L, prev_kv=()):
    t = x.shape[0]
    nt = L // tm
    n_prev = prev_kv[0].shape[0] if prev_kv else 0
    kv_block = lambda layers: pl.BlockSpec((layers, None, SB_WIDTH, tm), lambda i: (0, i // nt, 0, i % nt))
    kv_spec = kv_block(n_prev + 1)
    kv_sds = jax.ShapeDtypeStruct((n_prev + 1, n, SB_WIDTH, L), F32)
    row = lambda w: pl.BlockSpec((tm, w), lambda i: (i, 0))
    sds = lambda w, dt: jax.ShapeDtypeStruct((t, w), dt)
    widths = [(SB_WIDTH, BF16), (SB_WIDTH, F32), (SB_WIDTH, F32), (SB_WIDTH, BF16), (SB_WIDTH, BF16),
              (GLA_KEY_WIDTH, BF16), (GLA_KEY_WIDTH, BF16), (GLA_KEY_WIDTH, BF16), (GLA_WIDTH, BF16),
              (GLA_KEY_WIDTH, F32), (GLA_WIDTH, F32), (CM_WIDTH, BF16)]
    return pl.pallas_call(
        functools.partial(_mix_prompt_kernel, tm=tm, n_prev=n_prev),
        grid=(t // tm,),
        in_specs=[row(D_MODEL)] + _mix_weight_specs() + [
            _const_spec((tm, tm)), _const_spec((tm, tm)),
            _const_spec((CM_GROUPS, CM_CHUNK, CM_CHUNK)), _const_spec((CM_CHUNK, CM_WIDTH))]
                 + [kv_block(n_prev)] * len(prev_kv),
        out_specs=[kv_spec if i in (1, 2) else row(w) for i, (w, _) in enumerate(widths)],
        out_shape=[kv_sds if i in (1, 2) else sds(w, dt) for i, (w, dt) in enumerate(widths)],
        compiler_params=_params("parallel"),
        name="mix_prompt",
    )(x, *mw, cum, tot, w_s, bs_rows, *prev_kv)


def _mix_sample(x, mw, w00_row, b0_row):
    t = x.shape[0]
    full = lambda w: _const_spec((t, w))
    widths = [(SB_WIDTH, F32), (SB_WIDTH, F32), (SB_WIDTH, F32), (GLA_KEY_WIDTH, F32), (GLA_KEY_WIDTH, F32),
              (GLA_WIDTH, F32), (GLA_KEY_WIDTH, F32), (GLA_WIDTH, F32), (CM_WIDTH, F32), (CM_WIDTH, BF16)]
    return pl.pallas_call(
        _mix_sample_kernel,
        grid=(1,),
        in_specs=[full(D_MODEL)] + _mix_weight_specs() + [_const_spec((1, CM_WIDTH)), _const_spec((1, CM_WIDTH))],
        out_specs=[full(w) for w, _ in widths],
        out_shape=[jax.ShapeDtypeStruct((t, w), dt) for w, dt in widths],
        compiler_params=_params("arbitrary"),
        name="mix_sample",
    )(x, *mw, w00_row, b0_row)


def _sb_prompt_kernel(*refs, tq, hosted):
    hp = pl.program_id(1)
    qi = pl.program_id(2)
    if hosted is None:
        bias_ref, q_ref, k_ref, v_ref, u_ref, o_ref, acc_ref, z_ref, w_ref, c_ref, r_ref = refs
    else:
        pages, steps_per_row, past = hosted
        bias_ref, q_ref, k_ref, v_ref, u_ref = refs[1:6]
        s_in = refs[6:6 + N_SAMPLE_IN]
        pages_at = 6 + N_SAMPLE_IN
        s_k, s_v = refs[pages_at:pages_at + pages], refs[pages_at + pages:pages_at + 2 * pages]
        o_ref, s_o, acc_ref, z_ref, w_ref, c_ref, r_ref, s_qb, s_acc, s_carry = refs[pages_at + 2 * pages:]
        s_step = qi % steps_per_row
        _sample_sweep_step(s_step, None, *s_in, s_k, s_v, s_o, s_qb, s_acc, s_carry, past=past)
    nh = q_ref.shape[-1] // HEAD_DIM
    lane_head = lax.broadcasted_iota(jnp.int32, (1, LANES), 1) // HEAD_DIM
    pair_lanes = [slice((hh // 2) * LANES, (hh // 2 + 1) * LANES) for hh in range(nh)]
    lane = lax.broadcasted_iota(jnp.int32, (1, LANES), 1)
    q_heads, k_fill = [], []
    for hh in range(nh):
        free = (1 - hh % 2) * HEAD_DIM
        rest = jnp.full((1, LANES), bias_ref[nh * hp + hh] * LOG2E, F32)
        parts = jnp.zeros((1, LANES), F32)
        for i in range(3):
            part = rest.astype(BF16).astype(F32)
            parts = jnp.where(lane == free + i, part, parts)
            rest = rest - part
        fill = jnp.broadcast_to(parts, (tq, LANES)).astype(BF16)
        q_heads.append(jnp.where(lane_head == hh % 2, q_ref[:, pair_lanes[hh]], fill))
        ones = jnp.where(jnp.logical_and(lane >= free, lane < free + 3), 1.0, 0.0)
        k_fill.append(jnp.broadcast_to(ones, (tq, LANES)).astype(BF16))
    row = lax.broadcasted_iota(jnp.int32, (tq, tq), 0)
    col = lax.broadcasted_iota(jnp.int32, (tq, tq), 1)
    heads = [slice(hh * tq, (hh + 1) * tq) for hh in range(nh)]

    def scores(j):
        rows = pl.ds(pl.multiple_of(j * tq, tq), tq)
        return jnp.concatenate(
            [lax.dot_general(q_heads[hh], jnp.where(lane_head == hh % 2, k_ref[rows, pair_lanes[hh]], k_fill[hh]),
                             NT, preferred_element_type=F32) for hh in range(nh)], axis=0)

    def log_terms(p, diagonal):
        for hh in range(nh):
            z = z_ref[p, heads[hh], :]
            sp = _softplus2(z)
            if diagonal:
                sp = jnp.where(col < row, sp, 0.0)
            w = z - sp - _dot(sp.astype(BF16), u_ref[...])
            if diagonal:
                w = jnp.where(col < row, w, MASKED_LOG)
            w_ref[p, heads[hh], :] = w
            r_ref[heads[hh], :] = jnp.broadcast_to(jnp.sum(sp, axis=1, keepdims=True), (tq, LANES))

    def attend(j, p):
        rows = pl.ds(pl.multiple_of(j * tq, tq), tq)
        a = []
        for hh in range(nh):
            c = c_ref[p, heads[hh], :]
            a.append(jnp.exp2(w_ref[p, heads[hh], :] + jnp.concatenate([c] * (tq // LANES), axis=1)).astype(BF16))
        for hh in range(0, nh, 2):
            o2 = _dot(jnp.concatenate(a[hh:hh + 2], axis=0), v_ref[rows, pair_lanes[hh]])
            acc_ref[:, pair_lanes[hh]] += jnp.where(lane_head == 0, o2[:tq], o2[tq:])

    def step(n, p, with_attend=True):
        if with_attend:
            attend(qi - n + 2, p)
        c_ref[p] = c_ref[1 - p] - r_ref[...]
        z_ref[1 - p] = scores(jnp.maximum(qi - n - 1, 0))
        log_terms(p, False)

    acc_ref[...] = jnp.zeros_like(acc_ref)
    c_ref[0] = jnp.zeros(c_ref.shape[1:], F32)
    z_ref[0] = scores(qi)
    log_terms(0, True)
    z_ref[1] = scores(jnp.maximum(qi - 1, 0))

    @pl.when(qi >= 1)
    def _():
        step(1, 1, with_attend=False)

    left = jnp.maximum(qi - 1, 0)

    def body(i, _):
        for k in range(UNROLL):
            step(2 + UNROLL * i + k, k % 2)
        return 0

    lax.fori_loop(0, left // UNROLL, body, 0)
    nxt = 2 + (left // UNROLL) * UNROLL
    rem = left % UNROLL

    @pl.when(rem >= 2)
    def _():
        step(nxt, 0)
        step(nxt + 1, 1)

    @pl.when(rem % 2 == 1)
    def _():
        step(qi, 0)

    @pl.when(qi >= 1)
    def _():
        attend(1, (qi - 1) % 2)

    attend(0, qi % 2)
    o_ref[...] = acc_ref[...].astype(BF16)
    if hosted is not None:
        _sample_finish(s_step == steps_per_row - 1, s_o, s_acc)


def _sb_hosted_pages(n, L, tq, page_table):
    nb, n_pages = page_table.shape
    steps = n * (SB_HEADS // SB_STEP_HEADS) * (L // tq)
    if steps % nb:
        return None
    steps_per_row = steps // nb
    if n_pages % steps_per_row or (L // tq) % steps_per_row:
        return None
    return n_pages // steps_per_row


def _sb_prompt(q, k, v, bias, u, *, tq, sample=None):
    n, L, _ = q.shape
    nh, nq = SB_HEADS // SB_STEP_HEADS, L // tq
    width, rows_z = SB_STEP_HEADS * HEAD_DIM, SB_STEP_HEADS * tq
    blk = lambda rows: (None, rows, width)
    in_specs = [
        pl.BlockSpec(memory_space=pltpu.SMEM),
        pl.BlockSpec(blk(tq), lambda b, h, i, *_: (b, i, h)),
        pl.BlockSpec(blk(L), lambda b, h, i, *_: (b, 0, h)),
        pl.BlockSpec(blk(L), lambda b, h, i, *_: (b, 0, h)),
        pl.BlockSpec((tq, tq), lambda b, h, i, *_: (0, 0)),
    ]
    out_specs = pl.BlockSpec(blk(tq), lambda b, h, i, *_: (b, i, h))
    out_shape = jax.ShapeDtypeStruct((n, L, SB_WIDTH), BF16)
    scratch = [pltpu.VMEM((tq, width), F32), pltpu.VMEM((2, rows_z, tq), F32), pltpu.VMEM((2, rows_z, tq), F32),
               pltpu.VMEM((2, rows_z, LANES), F32), pltpu.VMEM((rows_z, LANES), F32)]
    if sample is None:
        return pl.pallas_call(
            functools.partial(_sb_prompt_kernel, tq=tq, hosted=None),
            grid=(n, nh, nq), in_specs=in_specs, out_specs=out_specs, out_shape=out_shape, scratch_shapes=scratch,
            compiler_params=_params("parallel", "parallel", "arbitrary"),
            name="sb_prompt",
        )(bias, q, k, v, u)

    page_table, s_q, s_k, s_v, s_bias, s_u, cache_kt, cache_vt, layer, pages = sample
    nb, n_pages = page_table.shape
    spr = n_pages // pages
    row_of = lambda b, h, i: (b * nh + h) * (nq // spr) + i // spr

    def page_spec(p):
        return pl.BlockSpec(
            (None, None, SB_HEADS, HEAD_DIM, PAGE_SIZE),
            lambda b, h, i, pt: (layer, pt[row_of(b, h, i), (spr - 1 - i % spr) * pages + p], 0, 0, 0))

    per_row = pl.BlockSpec((None, SB_HEADS, HEAD_DIM, 1), lambda b, h, i, pt: (row_of(b, h, i), 0, 0, 0))
    grid_spec = pltpu.PrefetchScalarGridSpec(
        num_scalar_prefetch=1,
        grid=(n, nh, nq),
        in_specs=in_specs + [per_row, per_row, per_row,
                             pl.BlockSpec((SB_HEADS, 1), lambda b, h, i, pt: (0, 0)),
                             pl.BlockSpec((PAGE_SIZE, PAGE_SIZE), lambda b, h, i, pt: (0, 0))]
                 + [page_spec(p) for p in range(pages)] * 2,
        out_specs=[out_specs, per_row],
        scratch_shapes=scratch + _sample_scratch(),
    )
    return pl.pallas_call(
        functools.partial(_sb_prompt_kernel, tq=tq, hosted=(pages, spr, n_pages * PAGE_SIZE)),
        grid_spec=grid_spec,
        out_shape=[out_shape, jax.ShapeDtypeStruct((nb, SB_HEADS, HEAD_DIM, 1), F32)],
        compiler_params=_params("arbitrary", "arbitrary", "arbitrary"),
        name="sb_prompt_hosting_sample",
    )(page_table, bias, q, k, v, u, s_q, s_k, s_v, s_bias, s_u, *([cache_kt] * pages), *([cache_vt] * pages))


def _gla_prompt_kernel(qt_ref, kt_ref, kd_ref, v_ref, eb_ref, sg_ref, gn_ref, hsum_ref,
                       og_ref, st_ref, s_ref, o_ref, *, tc):
    @pl.when(pl.program_id(1) == 0)
    def _():
        s_ref[...] = jnp.zeros_like(s_ref)

    ck = GLA_CHUNK
    key_head = lax.broadcasted_iota(jnp.int32, (1, GLA_KEY_WIDTH), 1) // GLA_DK
    val_head = lax.broadcasted_iota(jnp.int32, (1, GLA_WIDTH), 1) // GLA_DV
    state_mask = (lax.broadcasted_iota(jnp.int32, (GLA_WIDTH, GLA_KEY_WIDTH), 0) // GLA_DV
                  == lax.broadcasted_iota(jnp.int32, (GLA_WIDTH, GLA_KEY_WIDTH), 1) // GLA_DK)
    causal = (lax.broadcasted_iota(jnp.int32, (ck, ck), 0) >= lax.broadcasted_iota(jnp.int32, (ck, ck), 1))
    for c in range(tc // ck):
        rows = slice(c * ck, (c + 1) * ck)
        qt, kt, kd, v = qt_ref[rows, :], kt_ref[rows, :], kd_ref[rows, :], v_ref[rows, :]
        att = []
        v_bd = []
        for hh in range(GLA_HEADS):
            qh = jnp.where(key_head == hh, qt, jnp.zeros_like(qt))
            s = lax.dot_general(qh, kt, NT, preferred_element_type=F32)
            att.append(jnp.where(causal, s, 0.0).astype(BF16))
            v_bd.append(jnp.where(val_head == hh, v, jnp.zeros_like(v)))
        st = s_ref[...]
        o = _dot(jnp.concatenate(att, axis=1), jnp.concatenate(v_bd, axis=0))
        o = o + lax.dot_general(qt, st.astype(BF16), NT, preferred_element_type=F32)
        o_ref[rows, :] = o
        ds = lax.dot_general(v, kd, TN, preferred_element_type=F32)
        decay = eb_ref[c * ck + ck - 1:c * ck + ck, :]
        s_ref[...] = decay * st + jnp.where(state_mask, ds, 0.0)
    o = o_ref[...]
    ms = _dot((o * o).astype(BF16), hsum_ref[...]) * (1.0 / GLA_DV)
    og_ref[...] = (o * lax.rsqrt(ms + EPS) * gn_ref[...] * sg_ref[...]).astype(BF16)
    st_ref[...] = s_ref[...]


def _gla_prompt(qt, kt, kd, gv, eb, sg, gn_row, hsum, *, n, L, tc):
    t = n * L
    nc = L // tc
    row = lambda w: pl.BlockSpec((tc, w), lambda b, i: (b * nc + i, 0))
    return pl.pallas_call(
        functools.partial(_gla_prompt_kernel, tc=tc),
        grid=(n, nc),
        in_specs=[row(GLA_KEY_WIDTH), row(GLA_KEY_WIDTH), row(GLA_KEY_WIDTH), row(GLA_WIDTH),
                  row(GLA_KEY_WIDTH), row(GLA_WIDTH), _const_spec((1, GLA_WIDTH)),
                  _const_spec((GLA_WIDTH, GLA_WIDTH))],
        out_specs=[row(GLA_WIDTH), pl.BlockSpec((None, GLA_WIDTH, GLA_KEY_WIDTH), lambda b, i: (b, 0, 0))],
        out_shape=[jax.ShapeDtypeStruct((t, GLA_WIDTH), BF16),
                   jax.ShapeDtypeStruct((n, GLA_WIDTH, GLA_KEY_WIDTH), F32)],
        scratch_shapes=[pltpu.VMEM((GLA_WIDTH, GLA_KEY_WIDTH), F32), pltpu.VMEM((tc, GLA_WIDTH), F32)],
        compiler_params=_params("parallel", "arbitrary"),
        name="gla_prompt",
    )(qt, kt, kd, gv, eb, sg, gn_row, hsum)


def _sample_sweep_step(s, is_last, q_ref, kown_ref, vown_ref, bias_ref, u_ref, k_refs, v_refs, o_ref,
                       qb_ref, acc_ref, carry_ref, *, past):
    pages = len(k_refs)
    bias = bias_ref[...]

    @pl.when(s == 0)
    def _():
        q = q_ref[...]
        qb_ref[...] = jnp.broadcast_to(q, qb_ref.shape)
        key_pos = past + 0 * lax.broadcasted_iota(jnp.int32, (SB_HEADS, 1), 0)
        valid = key_pos < past
        z = jnp.sum(q * kown_ref[...], axis=1) + bias
        sp = jnp.where(valid, _softplus(z), 0.0)
        a = jnp.where(valid, jnp.exp(z - sp), 0.0)
        lane = lax.broadcasted_iota(jnp.int32, acc_ref.shape, 2)
        own = jnp.broadcast_to(a[:, :, None] * vown_ref[...], acc_ref.shape)
        acc_ref[...] = jnp.where(lane == 0, own, 0.0)
        carry_ref[...] = -sp

    order = list(reversed(range(pages)))
    qb = qb_ref[...]
    z = jnp.concatenate([jnp.sum(k_refs[p][...] * qb, axis=1) for p in order], axis=0)
    z = z + jnp.concatenate([bias] * pages, axis=0)
    sp = _softplus(z)
    hi = sp.astype(BF16).astype(F32)
    lo = (sp - hi).astype(BF16).astype(F32)
    later = _dot(hi, u_ref[...]) + _dot(lo, u_ref[...])
    totals = jnp.sum(sp, axis=1, keepdims=True)
    carries = [carry_ref[...]]
    for i in range(pages):
        carries.append(carries[-1] - totals[i * SB_HEADS:(i + 1) * SB_HEADS])
    carry_ref[...] = carries[-1]
    a = jnp.exp(z - sp - later + jnp.concatenate(carries[:-1], axis=0))
    for h in range(SB_HEADS):
        acc = acc_ref[h]
        for i, p in enumerate(order):
            acc = acc + a[i * SB_HEADS + h:i * SB_HEADS + h + 1, :] * v_refs[p][h]
        acc_ref[h] = acc
    if is_last is not None:
        _sample_finish(is_last, o_ref, acc_ref)


def _sample_finish(is_last, o_ref, acc_ref):
    @pl.when(is_last)
    def _():
        o_ref[...] = jnp.sum(acc_ref[...], axis=2, keepdims=True)


N_SAMPLE_IN = 5


def _sample_scratch():
    return [pltpu.VMEM((SB_HEADS, HEAD_DIM, PAGE_SIZE), F32), pltpu.VMEM((SB_HEADS, HEAD_DIM, PAGE_SIZE), F32),
            pltpu.VMEM((SB_HEADS, 1), F32)]


def _sb_sample_kernel(pt_ref, *refs, pages, past):
    ins, refs = refs[:N_SAMPLE_IN], refs[N_SAMPLE_IN:]
    k_refs, v_refs = refs[:pages], refs[pages:2 * pages]
    o_ref, qb_ref, acc_ref, carry_ref = refs[2 * pages:]
    s = pl.program_id(1)
    _sample_sweep_step(s, s == pl.num_programs(1) - 1, *ins, k_refs, v_refs, o_ref, qb_ref, acc_ref, carry_ref,
                       past=past)


def _sb_sample(page_table, q, k_own, v_own, bias_col, u, cache_kt, cache_vt, *, layer, pages):
    nb, n_pages = page_table.shape
    n_steps = n_pages // pages
    past = n_pages * PAGE_SIZE
    page_blk = (None, None, SB_HEADS, HEAD_DIM, PAGE_SIZE)

    def page_spec(p):
        return pl.BlockSpec(page_blk, lambda b, s, pt: (layer, pt[b, (n_steps - 1 - s) * pages + p], 0, 0, 0))

    per_b = pl.BlockSpec((None, SB_HEADS, HEAD_DIM, 1), lambda b, s, pt: (b, 0, 0, 0))
    grid_spec = pltpu.PrefetchScalarGridSpec(
        num_scalar_prefetch=1,
        grid=(nb, n_steps),
        in_specs=[per_b, per_b, per_b,
                  pl.BlockSpec((SB_HEADS, 1), lambda b, s, pt: (0, 0)),
                  pl.BlockSpec((PAGE_SIZE, PAGE_SIZE), lambda b, s, pt: (0, 0))]
                 + [page_spec(p) for p in range(pages)] * 2,
        out_specs=per_b,
        scratch_shapes=_sample_scratch(),
    )
    return pl.pallas_call(
        functools.partial(_sb_sample_kernel, pages=pages, past=past),
        grid_spec=grid_spec,
        out_shape=jax.ShapeDtypeStruct((nb, SB_HEADS, HEAD_DIM, 1), F32),
        compiler_params=_params("parallel", "arbitrary"),
        name="sb_sample",
    )(page_table, q, k_own, v_own, bias_col, u, *([cache_kt] * pages), *([cache_vt] * pages))


def _gla_sample_kernel(a_ref, s_ref, k_ref, v_ref, q_ref, sg_ref, gn_ref, s_out, og_out):
    s_new = a_ref[...] * s_ref[...] + k_ref[...] * v_ref[...]
    s_out[...] = s_new
    o = jnp.sum(q_ref[...] * s_new, axis=2)
    og_out[...] = _rms(o, gn_ref[...]) * sg_ref[...]


def _gla_sample(a, s0, gk, gv, gq, sg, gn):
    nb = s0.shape[0]
    col = lambda t: t.reshape(nb, GLA_HEADS, GLA_DK, 1)
    args = (col(a), s0, col(gk), gv.reshape(nb, GLA_HEADS, 1, GLA_DV), col(gq),
            sg.reshape(nb, GLA_HEADS, GLA_DV), gn.reshape(1, 1, GLA_DV))
    return pl.pallas_call(
        _gla_sample_kernel,
        out_shape=[jax.ShapeDtypeStruct(s0.shape, F32), jax.ShapeDtypeStruct((nb, GLA_HEADS, GLA_DV), F32)],
        name="gla_sample",
    )(*args)


def _tile(t, target):
    return target if t % target == 0 else t


def kernel(x_prompt, x_sample, cache_k, cache_v, state_gla, page_table, ffn1_norm_g, ffn1_w_up, ffn1_w_down, mix_norm_g, w_in, q_norm_g, k_norm_g, sb_logit_bias, gla_w_gate2, gla_b_gate, gla_out_norm_g, cm_ln_g, cm_ln_b, cm_w_spatial, cm_b_spatial, w_out, ffn2_norm_g, ffn2_w_up, ffn2_w_down):
    nb_p, L, d = x_prompt.shape
    nb_s, n_dec, _ = x_sample.shape
    depth = w_in.shape[0]
    assert n_dec == 1 and d == D_MODEL and L % CM_CHUNK == 0
    tp = nb_p * L
    tm = _tile(L, 512)
    tq = _tile(L, 256)
    tf = D_FF // 2
    hosted_pages = _sb_hosted_pages(nb_p, L, tq, page_table)
    pages = hosted_pages or (8 if page_table.shape[1] % 8 == 0 else 1)

    idx = jnp.arange(tm)
    same_chunk = (idx[:, None] // GLA_CHUNK) == (idx[None, :] // GLA_CHUNK)
    cum = (same_chunk & (idx[None, :] <= idx[:, None])).astype(BF16)
    tot = same_chunk.astype(BF16)
    hsum512 = ((jnp.arange(SB_WIDTH)[:, None] // HEAD_DIM) == (jnp.arange(SB_WIDTH)[None, :] // HEAD_DIM)).astype(BF16)
    hsum256 = hsum512[:GLA_WIDTH, :GLA_WIDTH]
    iq = jnp.arange(tq)
    u_later = (iq[:, None] > iq[None, :]).astype(BF16)
    ip = jnp.arange(PAGE_SIZE)
    u_page = (ip[:, None] > ip[None, :]).astype(F32)
    cache_kt = jnp.transpose(cache_k, (0, 1, 3, 4, 2))
    cache_vt = jnp.transpose(cache_v, (0, 1, 3, 4, 2))

    xp = x_prompt.reshape(tp, d)
    xs = x_sample.reshape(nb_s * n_dec, d)
    row = lambda v: v.reshape(1, -1).astype(F32)
    outs = {k: [] for k in ("sp", "ks", "vs", "ss", "cv")}
    kv_stack = ()
    for l in range(depth):
        w = w_in[l]
        w_r = jnp.concatenate([w[:, :C_GG], w[:, C_GG + GLA_RANK:], w[:, C_GG:C_GG + GLA_RANK],
                               jnp.zeros((d, N_IN_PAD - w.shape[1]), w.dtype)], axis=1).astype(BF16)
        wg2 = jnp.concatenate([gla_w_gate2[l], jnp.zeros((LANES - GLA_RANK, GLA_KEY_WIDTH), F32)], axis=0)
        mw = (row(mix_norm_g[l]), w_r, row(jnp.tile(q_norm_g[l], SB_HEADS)), row(jnp.tile(k_norm_g[l], SB_HEADS)),
              wg2, row(gla_b_gate[l]), row(cm_ln_g[l]), row(cm_ln_b[l]), hsum512)
        f1 = (row(ffn1_norm_g[l]), ffn1_w_up[l].astype(BF16), ffn1_w_down[l].astype(BF16))
        f2 = (row(ffn2_norm_g[l]), ffn2_w_up[l].astype(BF16), ffn2_w_down[l].astype(BF16))
        wo = w_out[l].astype(BF16)
        gn_row = row(jnp.tile(gla_out_norm_g[l], GLA_HEADS))
        bs_rows = jnp.repeat(cm_b_spatial[l].T, CM_GROUP_DIM, axis=1)

        xp = _ffn(xp, *f1, tm=tm, tf=tf)
        (q, kf, vf, kb, vb, qt, kt, kd, gv, eb, sg, ocm) = _mix_prompt(xp, mw, cum, tot, cm_w_spatial[l], bs_rows, tm=tm,
                                                                         n=nb_p, L=L, prev_kv=kv_stack)
        kv_stack = (kf, vf)
        ts = xs.shape[0]
        xs = _ffn(xs, *f1, tm=ts, tf=tf)
        w00_row = row(jnp.repeat(cm_w_spatial[l][:, 0, 0], CM_GROUP_DIM))
        b0_row = row(jnp.repeat(cm_b_spatial[l][:, 0], CM_GROUP_DIM))
        (q_s, k_s, v_s, gq_s, gk_s, gv_s, a_s, sg_s, cv_s, ocm_s) = _mix_sample(xs, mw, w00_row, b0_row)
        col4 = lambda t: t.reshape(nb_s, SB_HEADS, HEAD_DIM, 1)
        sample = (page_table, col4(q_s), col4(k_s), col4(v_s), sb_logit_bias[l].reshape(SB_HEADS, 1), u_page,
                  cache_kt, cache_vt, l, pages)

        r3 = lambda t: t.reshape(nb_p, L, SB_WIDTH)
        if hosted_pages:
            o_sb, o_sb_s = _sb_prompt(r3(q), r3(kb), r3(vb), sb_logit_bias[l], u_later, tq=tq, sample=sample)
        else:
            o_sb = _sb_prompt(r3(q), r3(kb), r3(vb), sb_logit_bias[l], u_later, tq=tq)
            o_sb_s = _sb_sample(*sample[:-2], layer=l, pages=pages)
        o_sb = o_sb.reshape(tp, SB_WIDTH)
        o_gla, st = _gla_prompt(qt, kt, kd, gv, eb, sg, gn_row, hsum256, n=nb_p, L=L, tc=tm)
        xp = _ffn(xp, *f2, tm=tm, tf=tf, merge=(o_sb, o_gla, ocm, wo))
        st5 = st.reshape(nb_p, GLA_HEADS, GLA_DV, GLA_HEADS, GLA_DK)
        outs["sp"].append(jnp.stack([st5[:, h, :, h, :] for h in range(GLA_HEADS)], axis=1).transpose(0, 1, 3, 2))

        s_new, og = _gla_sample(a_s, state_gla[l], gk_s, gv_s, gq_s, sg_s, gla_out_norm_g[l])
        xs = _ffn(xs, *f2, tm=ts, tf=tf,
                  merge=(o_sb_s.reshape(ts, SB_WIDTH).astype(BF16), og.reshape(ts, GLA_WIDTH).astype(BF16), ocm_s, wo))
        outs["ks"].append(k_s.reshape(nb_s, n_dec, SB_HEADS, HEAD_DIM))
        outs["vs"].append(v_s.reshape(nb_s, n_dec, SB_HEADS, HEAD_DIM))
        outs["ss"].append(s_new)
        outs["cv"].append(cv_s.reshape(nb_s, n_dec, CM_GROUPS, CM_GROUP_DIM))

    st = lambda k: jnp.stack(outs[k])
    to_cache = lambda t: t.reshape(depth, nb_p, SB_HEADS, HEAD_DIM, L).transpose(0, 1, 4, 2, 3)
    return (xp.reshape(nb_p, L, d), xs.reshape(nb_s, n_dec, d), to_cache(kv_stack[0]), to_cache(kv_stack[1]),
            st("sp"), st("ks"), st("vs"), st("ss"), st("cv"))
```

```python
import functools

import jax
import jax.numpy as jnp
from jax import lax
from jax.experimental import pallas as pl
from jax.experimental.pallas import tpu as pltpu

F32 = jnp.float32
BF16 = jnp.bfloat16

D_MODEL = 1024
HEAD_DIM = 64
SB_HEADS = 8
SB_WIDTH = SB_HEADS * HEAD_DIM
GLA_HEADS = 4
GLA_DV = 64
GLA_DK = 32
GLA_WIDTH = GLA_HEADS * GLA_DV
GLA_KEY_WIDTH = GLA_HEADS * GLA_DK
GLA_RANK = 16
GLA_TAU = 16.0
GLA_CHUNK = 64
CM_GROUPS = 4
CM_WIDTH = 256
CM_GROUP_DIM = 64
CM_CHUNK = 128
D_FF = 2816
EPS = 1e-6
PAGE_SIZE = 128

LOG2E = 1.4426950408889634
MASKED_LOG = -1e30
UNROLL = 4
SB_STEP_HEADS = 2
LANES = 128
VMEM_LIMIT = 56 * 1024 * 1024

C_SQ, C_SK, C_SV = 0, 512, 1024
C_GQ, C_GK, C_GV, C_GG = 1536, 1664, 1792, 2048
C_CU, C_CV, C_LR = 2304, 2560, 2816
N_IN_PAD = 2944

NT = (((1,), (1,)), ((), ()))
TN = (((0,), (0,)), ((), ()))


def _params(*sem):
    return pltpu.CompilerParams(dimension_semantics=sem, vmem_limit_bytes=VMEM_LIMIT)


def _const_spec(shape):
    n = len(shape)
    return pl.BlockSpec(shape, lambda *_: (0,) * n)


def _softplus(z):
    return jnp.maximum(z, 0.0) + jnp.log1p(jnp.exp(-jnp.abs(z)))


def _softplus2(x):
    neg_abs = pltpu.bitcast(pltpu.bitcast(x, jnp.uint32) | jnp.uint32(0x80000000), F32)
    return jnp.maximum(x, 0.0) + jnp.log2(1.0 + jnp.exp2(neg_abs))


def _split_bf16(x):
    hi = x.astype(BF16)
    lo = (x - hi.astype(F32)).astype(BF16)
    return hi, lo


def _dot(a, b):
    return jnp.dot(a, b, preferred_element_type=F32)


def _rms(x, g_row):
    return x * lax.rsqrt(jnp.mean(x * x, axis=-1, keepdims=True) + EPS) * g_row


def _ffn_kernel(*refs, nf, merged):
    if merged:
        x_ref, sb_ref, gla_ref, cm_ref, wo_ref, g_ref, wg_ref, wu_ref, wd_ref, o_ref, h_ref, acc_ref, x1_ref = refs
    else:
        x_ref, g_ref, wg_ref, wu_ref, wd_ref, o_ref, h_ref, acc_ref = refs
        x1_ref = x_ref
    f = pl.program_id(1)

    @pl.when(f == 0)
    def _():
        if merged:
            o = _dot(sb_ref[...], wo_ref[0:SB_WIDTH, :])
            o = o + _dot(gla_ref[...], wo_ref[SB_WIDTH:SB_WIDTH + GLA_WIDTH, :])
            o = o + _dot(cm_ref[...], wo_ref[SB_WIDTH + GLA_WIDTH:, :])
            x1_ref[...] = x_ref[...] + o
        h_ref[...] = _rms(x1_ref[...], g_ref[...]).astype(BF16)
        acc_ref[...] = jnp.zeros_like(acc_ref)

    h = h_ref[...]
    gate = _dot(h, wg_ref[...])
    up = _dot(h, wu_ref[...])
    act = (gate * jax.nn.sigmoid(gate) * up).astype(BF16)
    acc_ref[...] += _dot(act, wd_ref[...])

    @pl.when(f == nf - 1)
    def _():
        o_ref[...] = x1_ref[...] + 0.5 * acc_ref[...]


def _ffn(x, g_row, w_up, w_down, *, tm, tf, merge=None):
    t, d = x.shape
    nf = D_FF // tf
    row = lambda w: pl.BlockSpec((tm, w), lambda i, f: (i, 0))
    merge_specs, merge_args, scratch = [], (), []
    if merge is not None:
        merge_specs = [row(SB_WIDTH), row(GLA_WIDTH), row(CM_WIDTH), _const_spec((d, d))]
        merge_args = tuple(merge)
        scratch = [pltpu.VMEM((tm, d), F32)]
    return pl.pallas_call(
        functools.partial(_ffn_kernel, nf=nf, merged=merge is not None),
        grid=(t // tm, nf),
        in_specs=[row(d)] + merge_specs + [
            _const_spec((1, d)),
            pl.BlockSpec((d, tf), lambda i, f: (0, f)),
            pl.BlockSpec((d, tf), lambda i, f: (0, f + nf)),
            pl.BlockSpec((tf, d), lambda i, f: (f, 0)),
        ],
        out_specs=row(d),
        out_shape=jax.ShapeDtypeStruct((t, d), F32),
        scratch_shapes=[pltpu.VMEM((tm, d), BF16), pltpu.VMEM((tm, d), F32)] + scratch,
        compiler_params=_params("parallel", "arbitrary"),
        name="half_ffn_merged" if merge is not None else "half_ffn",
    )(x, *merge_args, g_row, w_up, w_up, w_down)


def _mix_common(x_ref, g_ref, w_ref, qg_ref, kg_ref, wg2_ref, bg_ref, lng_ref, lnb_ref, hsum_ref):
    h = _rms(x_ref[...], g_ref[...]).astype(BF16)

    def proj(lo, width):
        return _dot(h, w_ref[:, lo:lo + width])

    def head_norm(t, gain_row):
        ms = _dot((t * t).astype(BF16), hsum_ref[...]) * (1.0 / HEAD_DIM)
        return t * lax.rsqrt(ms + EPS) * gain_row

    out = {}
    out["q"] = head_norm(proj(C_SQ, SB_WIDTH), qg_ref[...]) * (HEAD_DIM ** -0.5)
    out["k"] = head_norm(proj(C_SK, SB_WIDTH), kg_ref[...])
    out["v"] = proj(C_SV, SB_WIDTH)
    out["gq"] = proj(C_GQ, GLA_KEY_WIDTH) * (GLA_DK ** -0.5)
    out["gk"] = proj(C_GK, GLA_KEY_WIDTH)
    out["gv"] = proj(C_GV, GLA_WIDTH)
    gg = proj(C_GG, GLA_WIDTH)
    out["sg"] = gg * jax.nn.sigmoid(gg)
    lr_hi, lr_lo = _split_bf16(proj(C_LR, LANES))
    w_hi, w_lo = _split_bf16(wg2_ref[...])
    pre = _dot(lr_hi, w_hi) + _dot(lr_lo, w_hi) + _dot(lr_hi, w_lo) + bg_ref[...]
    out["la"] = -_softplus(-pre) * (1.0 / GLA_TAU)
    out["cu"] = jax.nn.gelu(proj(C_CU, CM_WIDTH))
    c = jax.nn.gelu(proj(C_CV, CM_WIDTH))
    c = c - jnp.mean(c, axis=-1, keepdims=True)
    out["cv"] = c * lax.rsqrt(jnp.mean(c * c, axis=-1, keepdims=True) + EPS) * lng_ref[...] + lnb_ref[...]
    return out


def _mix_prompt_kernel(x_ref, g_ref, w_ref, qg_ref, kg_ref, wg2_ref, bg_ref, lng_ref, lnb_ref, hsum_ref,
                       cum_ref, tot_ref, ws_ref, bs_ref, *refs, tm, n_prev):
    prev_k, prev_v = refs[:2] if n_prev else (None, None)
    q_o, kf_o, vf_o, kb_o, vb_o, qt_o, kt_o, kd_o, gv_o, eb_o, sg_o, ocm_o = refs[2 if n_prev else 0:]
    m = _mix_common(x_ref, g_ref, w_ref, qg_ref, kg_ref, wg2_ref, bg_ref, lng_ref, lnb_ref, hsum_ref)
    q_o[...] = (m["q"] * LOG2E).astype(BF16)
    if n_prev:
        kf_o[0:n_prev] = prev_k[...]
        vf_o[0:n_prev] = prev_v[...]
    kf_o[n_prev] = m["k"].T
    vf_o[n_prev] = m["v"].T
    kb_o[...] = m["k"].astype(BF16)
    vb_o[...] = m["v"].astype(BF16)
    gv_o[...] = m["gv"].astype(BF16)
    sg_o[...] = m["sg"]
    la_split = jnp.concatenate(_split_bf16(m["la"]), axis=1)
    b = _dot(cum_ref[...], la_split)
    b = b[:, :GLA_KEY_WIDTH] + b[:, GLA_KEY_WIDTH:]
    b_last = _dot(tot_ref[...], la_split)
    b_last = b_last[:, :GLA_KEY_WIDTH] + b_last[:, GLA_KEY_WIDTH:]
    eb = jnp.exp(b)
    eb_o[...] = eb
    qt_o[...] = (m["gq"] * eb).astype(BF16)
    kt_o[...] = (m["gk"] * jnp.exp(-b)).astype(BF16)
    kd_o[...] = (m["gk"] * jnp.exp(b_last - b)).astype(BF16)
    row = lax.broadcasted_iota(jnp.int32, (CM_CHUNK, CM_CHUNK), 0)
    col = lax.broadcasted_iota(jnp.int32, (CM_CHUNK, CM_CHUNK), 1)
    group = lax.broadcasted_iota(jnp.int32, (1, CM_WIDTH), 1) // CM_GROUP_DIM
    w_tril = [jnp.where(row >= col, ws_ref[g], 0.0).astype(BF16) for g in range(CM_GROUPS)]
    cv = m["cv"].astype(BF16)
    for c in range(tm // CM_CHUNK):
        rows = slice(c * CM_CHUNK, (c + 1) * CM_CHUNK)
        mixed = bs_ref[...]
        for g in range(CM_GROUPS):
            mixed = mixed + _dot(w_tril[g], jnp.where(group == g, cv[rows], jnp.zeros_like(cv[rows])))
        ocm_o[rows, :] = (m["cu"][rows] * mixed).astype(BF16)


def _mix_sample_kernel(x_ref, g_ref, w_ref, qg_ref, kg_ref, wg2_ref, bg_ref, lng_ref, lnb_ref, hsum_ref,
                       w00_ref, b0_ref,
                       q_o, k_o, v_o, gq_o, gk_o, gv_o, a_o, sg_o, cv_o, ocm_o):
    m = _mix_common(x_ref, g_ref, w_ref, qg_ref, kg_ref, wg2_ref, bg_ref, lng_ref, lnb_ref, hsum_ref)
    q_o[...] = m["q"]
    k_o[...] = m["k"]
    v_o[...] = m["v"]
    gq_o[...] = m["gq"]
    gk_o[...] = m["gk"]
    gv_o[...] = m["gv"]
    a_o[...] = jnp.exp(m["la"])
    sg_o[...] = m["sg"]
    cv_o[...] = m["cv"]
    ocm_o[...] = (m["cu"] * (w00_ref[...] * m["cv"] + b0_ref[...])).astype(BF16)


def _mix_weight_specs():
    return [
        _const_spec((1, D_MODEL)),
        _const_spec((D_MODEL, N_IN_PAD)),
        _const_spec((1, SB_WIDTH)),
        _const_spec((1, SB_WIDTH)),
        _const_spec((LANES, GLA_KEY_WIDTH)),
        _const_spec((1, GLA_KEY_WIDTH)),
        _const_spec((1, CM_WIDTH)),
        _const_spec((1, CM_WIDTH)),
        _const_spec((SB_WIDTH, SB_WIDTH)),
    ]


def _mix_prompt(x, mw, cum, tot, w_s, bs_rows, *, tm, n, L, prev_kv=()):
    t = x.shape[0]
    nt = L // tm
    n_prev = prev_kv[0].shape[0] if prev_kv else 0
    kv_block = lambda layers: pl.BlockSpec((layers, None, SB_WIDTH, tm), lambda i: (0, i // nt, 0, i % nt))
    kv_spec = kv_block(n_prev + 1)
    kv_sds = jax.ShapeDtypeStruct((n_prev + 1, n, SB_WIDTH, L), F32)
    row = lambda w: pl.BlockSpec((tm, w), lambda i: (i, 0))
    sds = lambda w, dt: jax.ShapeDtypeStruct((t, w), dt)
    widths = [(SB_WIDTH, BF16), (SB_WIDTH, F32), (SB_WIDTH, F32), (SB_WIDTH, BF16), (SB_WIDTH, BF16),
              (GLA_KEY_WIDTH, BF16), (GLA_KEY_WIDTH, BF16), (GLA_KEY_WIDTH, BF16), (GLA_WIDTH, BF16),
              (GLA_KEY_WIDTH, F32), (GLA_WIDTH, F32), (CM_WIDTH, BF16)]
    return pl.pallas_call(
        functools.partial(_mix_prompt_kernel, tm=tm, n_prev=n_prev),
        grid=(t // tm,),
        in_specs=[row(D_MODEL)] + _mix_weight_specs() + [
            _const_spec((tm, tm)), _const_spec((tm, tm)),
            _const_spec((CM_GROUPS, CM_CHUNK, CM_CHUNK)), _const_spec((CM_CHUNK, CM_WIDTH))]
                 + [kv_block(n_prev)] * len(prev_kv),
        out_specs=[kv_spec if i in (1, 2) else row(w) for i, (w, _) in enumerate(widths)],
        out_shape=[kv_sds if i in (1, 2) else sds(w, dt) for i, (w, dt) in enumerate(widths)],
        compiler_params=_params("parallel"),
        name="mix_prompt",
    )(x, *mw, cum, tot, w_s, bs_rows, *prev_kv)


def _mix_sample(x, mw, w00_row, b0_row):
    t = x.shape[0]
    full = lambda w: _const_spec((t, w))
    widths = [(SB_WIDTH, F32), (SB_WIDTH, F32), (SB_WIDTH, F32), (GLA_KEY_WIDTH, F32), (GLA_KEY_WIDTH, F32),
              (GLA_WIDTH, F32), (GLA_KEY_WIDTH, F32), (GLA_WIDTH, F32), (CM_WIDTH, F32), (CM_WIDTH, BF16)]
    return pl.pallas_call(
        _mix_sample_kernel,
        grid=(1,),
        in_specs=[full(D_MODEL)] + _mix_weight_specs() + [_const_spec((1, CM_WIDTH)), _const_spec((1, CM_WIDTH))],
        out_specs=[full(w) for w, _ in widths],
        out_shape=[jax.ShapeDtypeStruct((t, w), dt) for w, dt in widths],
        compiler_params=_params("arbitrary"),
        name="mix_sample",
    )(x, *mw, w00_row, b0_row)


def _sb_prompt_kernel(*refs, tq, hosted):
    hp = pl.program_id(1)
    qi = pl.program_id(2)
    if hosted is None:
        bias_ref, q_ref, k_ref, v_ref, u_ref, o_ref, acc_ref, z_ref, w_ref, c_ref, r_ref = refs
    else:
        pages, steps_per_row, past = hosted
        bias_ref, q_ref, k_ref, v_ref, u_ref = refs[1:6]
        s_in = refs[6:6 + N_SAMPLE_IN]
        pages_at = 6 + N_SAMPLE_IN
        s_k, s_v = refs[pages_at:pages_at + pages], refs[pages_at + pages:pages_at + 2 * pages]
        o_ref, s_o, acc_ref, z_ref, w_ref, c_ref, r_ref, s_qb, s_acc, s_carry = refs[pages_at + 2 * pages:]
        s_step = qi % steps_per_row
        _sample_sweep_step(s_step, None, *s_in, s_k, s_v, s_o, s_qb, s_acc, s_carry, past=past)
    nh = q_ref.shape[-1] // HEAD_DIM
    lane_head = lax.broadcasted_iota(jnp.int32, (1, LANES), 1) // HEAD_DIM
    pair_lanes = [slice((hh // 2) * LANES, (hh // 2 + 1) * LANES) for hh in range(nh)]
    lane = lax.broadcasted_iota(jnp.int32, (1, LANES), 1)
    q_heads, k_fill = [], []
    for hh in range(nh):
        free = (1 - hh % 2) * HEAD_DIM
        rest = jnp.full((1, LANES), bias_ref[nh * hp + hh] * LOG2E, F32)
        parts = jnp.zeros((1, LANES), F32)
        for i in range(3):
            part = rest.astype(BF16).astype(F32)
            parts = jnp.where(lane == free + i, part, parts)
            rest = rest - part
        fill = jnp.broadcast_to(parts, (tq, LANES)).astype(BF16)
        q_heads.append(jnp.where(lane_head == hh % 2, q_ref[:, pair_lanes[hh]], fill))
        ones = jnp.where(jnp.logical_and(lane >= free, lane < free + 3), 1.0, 0.0)
        k_fill.append(jnp.broadcast_to(ones, (tq, LANES)).astype(BF16))
    row = lax.broadcasted_iota(jnp.int32, (tq, tq), 0)
    col = lax.broadcasted_iota(jnp.int32, (tq, tq), 1)
    heads = [slice(hh * tq, (hh + 1) * tq) for hh in range(nh)]

    def scores(j):
        rows = pl.ds(pl.multiple_of(j * tq, tq), tq)
        return jnp.concatenate(
            [lax.dot_general(q_heads[hh], jnp.where(lane_head == hh % 2, k_ref[rows, pair_lanes[hh]], k_fill[hh]),
                             NT, preferred_element_type=F32) for hh in range(nh)], axis=0)

    def log_terms(p, diagonal):
        for hh in range(nh):
            z = z_ref[p, heads[hh], :]
            sp = _softplus2(z)
            if diagonal:
                sp = jnp.where(col < row, sp, 0.0)
            w = z - sp - _dot(sp.astype(BF16), u_ref[...])
            if diagonal:
                w = jnp.where(col < row, w, MASKED_LOG)
            w_ref[p, heads[hh], :] = w
            r_ref[heads[hh], :] = jnp.broadcast_to(jnp.sum(sp, axis=1, keepdims=True), (tq, LANES))

    def attend(j, p):
        rows = pl.ds(pl.multiple_of(j * tq, tq), tq)
        a = []
        for hh in range(nh):
            c = c_ref[p, heads[hh], :]
            a.append(jnp.exp2(w_ref[p, heads[hh], :] + jnp.concatenate([c] * (tq // LANES), axis=1)).astype(BF16))
        for hh in range(0, nh, 2):
            o2 = _dot(jnp.concatenate(a[hh:hh + 2], axis=0), v_ref[rows, pair_lanes[hh]])
            acc_ref[:, pair_lanes[hh]] += jnp.where(lane_head == 0, o2[:tq], o2[tq:])

    def step(n, p, with_attend=True):
        if with_attend:
            attend(qi - n + 2, p)
        c_ref[p] = c_ref[1 - p] - r_ref[...]
        z_ref[1 - p] = scores(jnp.maximum(qi - n - 1, 0))
        log_terms(p, False)

    acc_ref[...] = jnp.zeros_like(acc_ref)
    c_ref[0] = jnp.zeros(c_ref.shape[1:], F32)
    z_ref[0] = scores(qi)
    log_terms(0, True)
    z_ref[1] = scores(jnp.maximum(qi - 1, 0))

    @pl.when(qi >= 1)
    def _():
        step(1, 1, with_attend=False)

    left = jnp.maximum(qi - 1, 0)

    def body(i, _):
        for k in range(UNROLL):
            step(2 + UNROLL * i + k, k % 2)
        return 0

    lax.fori_loop(0, left // UNROLL, body, 0)
    nxt = 2 + (left // UNROLL) * UNROLL
    rem = left % UNROLL

    @pl.when(rem >= 2)
    def _():
        step(nxt, 0)
        step(nxt + 1, 1)

    @pl.when(rem % 2 == 1)
    def _():
        step(qi, 0)

    @pl.when(qi >= 1)
    def _():
        attend(1, (qi - 1) % 2)

    attend(0, qi % 2)
    o_ref[...] = acc_ref[...].astype(BF16)
    if hosted is not None:
        _sample_finish(s_step == steps_per_row - 1, s_o, s_acc)


def _sb_hosted_pages(n, L, tq, page_table):
    nb, n_pages = page_table.shape
    steps = n * (SB_HEADS // SB_STEP_HEADS) * (L // tq)
    if steps % nb:
        return None
    steps_per_row = steps // nb
    if n_pages % steps_per_row or (L // tq) % steps_per_row:
        return None
    return n_pages // steps_per_row


def _sb_prompt(q, k, v, bias, u, *, tq, sample=None):
    n, L, _ = q.shape
    nh, nq = SB_HEADS // SB_STEP_HEADS, L // tq
    width, rows_z = SB_STEP_HEADS * HEAD_DIM, SB_STEP_HEADS * tq
    blk = lambda rows: (None, rows, width)
    in_specs = [
        pl.BlockSpec(memory_space=pltpu.SMEM),
        pl.BlockSpec(blk(tq), lambda b, h, i, *_: (b, i, h)),
        pl.BlockSpec(blk(L), lambda b, h, i, *_: (b, 0, h)),
        pl.BlockSpec(blk(L), lambda b, h, i, *_: (b, 0, h)),
        pl.BlockSpec((tq, tq), lambda b, h, i, *_: (0, 0)),
    ]
    out_specs = pl.BlockSpec(blk(tq), lambda b, h, i, *_: (b, i, h))
    out_shape = jax.ShapeDtypeStruct((n, L, SB_WIDTH), BF16)
    scratch = [pltpu.VMEM((tq, width), F32), pltpu.VMEM((2, rows_z, tq), F32), pltpu.VMEM((2, rows_z, tq), F32),
               pltpu.VMEM((2, rows_z, LANES), F32), pltpu.VMEM((rows_z, LANES), F32)]
    if sample is None:
        return pl.pallas_call(
            functools.partial(_sb_prompt_kernel, tq=tq, hosted=None),
            grid=(n, nh, nq), in_specs=in_specs, out_specs=out_specs, out_shape=out_shape, scratch_shapes=scratch,
            compiler_params=_params("parallel", "parallel", "arbitrary"),
            name="sb_prompt",
        )(bias, q, k, v, u)

    page_table, s_q, s_k, s_v, s_bias, s_u, cache_kt, cache_vt, layer, pages = sample
    nb, n_pages = page_table.shape
    spr = n_pages // pages
    row_of = lambda b, h, i: (b * nh + h) * (nq // spr) + i // spr

    def page_spec(p):
        return pl.BlockSpec(
            (None, None, SB_HEADS, HEAD_DIM, PAGE_SIZE),
            lambda b, h, i, pt: (layer, pt[row_of(b, h, i), (spr - 1 - i % spr) * pages + p], 0, 0, 0))

    per_row = pl.BlockSpec((None, SB_HEADS, HEAD_DIM, 1), lambda b, h, i, pt: (row_of(b, h, i), 0, 0, 0))
    grid_spec = pltpu.PrefetchScalarGridSpec(
        num_scalar_prefetch=1,
        grid=(n, nh, nq),
        in_specs=in_specs + [per_row, per_row, per_row,
                             pl.BlockSpec((SB_HEADS, 1), lambda b, h, i, pt: (0, 0)),
                             pl.BlockSpec((PAGE_SIZE, PAGE_SIZE), lambda b, h, i, pt: (0, 0))]
                 + [page_spec(p) for p in range(pages)] * 2,
        out_specs=[out_specs, per_row],
        scratch_shapes=scratch + _sample_scratch(),
    )
    return pl.pallas_call(
        functools.partial(_sb_prompt_kernel, tq=tq, hosted=(pages, spr, n_pages * PAGE_SIZE)),
        grid_spec=grid_spec,
        out_shape=[out_shape, jax.ShapeDtypeStruct((nb, SB_HEADS, HEAD_DIM, 1), F32)],
        compiler_params=_params("arbitrary", "arbitrary", "arbitrary"),
        name="sb_prompt_hosting_sample",
    )(page_table, bias, q, k, v, u, s_q, s_k, s_v, s_bias, s_u, *([cache_kt] * pages), *([cache_vt] * pages))


def _gla_prompt_kernel(qt_ref, kt_ref, kd_ref, v_ref, eb_ref, sg_ref, gn_ref, hsum_ref,
                       og_ref, st_ref, s_ref, o_ref, *, tc):
    @pl.when(pl.program_id(1) == 0)
    def _():
        s_ref[...] = jnp.zeros_like(s_ref)

    ck = GLA_CHUNK
    key_head = lax.broadcasted_iota(jnp.int32, (1, GLA_KEY_WIDTH), 1) // GLA_DK
    val_head = lax.broadcasted_iota(jnp.int32, (1, GLA_WIDTH), 1) // GLA_DV
    state_mask = (lax.broadcasted_iota(jnp.int32, (GLA_WIDTH, GLA_KEY_WIDTH), 0) // GLA_DV
                  == lax.broadcasted_iota(jnp.int32, (GLA_WIDTH, GLA_KEY_WIDTH), 1) // GLA_DK)
    causal = (lax.broadcasted_iota(jnp.int32, (ck, ck), 0) >= lax.broadcasted_iota(jnp.int32, (ck, ck), 1))
    for c in range(tc // ck):
        rows = slice(c * ck, (c + 1) * ck)
        qt, kt, kd, v = qt_ref[rows, :], kt_ref[rows, :], kd_ref[rows, :], v_ref[rows, :]
        att = []
        v_bd = []
        for hh in range(GLA_HEADS):
            qh = jnp.where(key_head == hh, qt, jnp.zeros_like(qt))
            s = lax.dot_general(qh, kt, NT, preferred_element_type=F32)
            att.append(jnp.where(causal, s, 0.0).astype(BF16))
            v_bd.append(jnp.where(val_head == hh, v, jnp.zeros_like(v)))
        st = s_ref[...]
        o = _dot(jnp.concatenate(att, axis=1), jnp.concatenate(v_bd, axis=0))
        o = o + lax.dot_general(qt, st.astype(BF16), NT, preferred_element_type=F32)
        o_ref[rows, :] = o
        ds = lax.dot_general(v, kd, TN, preferred_element_type=F32)
        decay = eb_ref[c * ck + ck - 1:c * ck + ck, :]
        s_ref[...] = decay * st + jnp.where(state_mask, ds, 0.0)
    o = o_ref[...]
    ms = _dot((o * o).astype(BF16), hsum_ref[...]) * (1.0 / GLA_DV)
    og_ref[...] = (o * lax.rsqrt(ms + EPS) * gn_ref[...] * sg_ref[...]).astype(BF16)
    st_ref[...] = s_ref[...]


def _gla_prompt(qt, kt, kd, gv, eb, sg, gn_row, hsum, *, n, L, tc):
    t = n * L
    nc = L // tc
    row = lambda w: pl.BlockSpec((tc, w), lambda b, i: (b * nc + i, 0))
    return pl.pallas_call(
        functools.partial(_gla_prompt_kernel, tc=tc),
        grid=(n, nc),
        in_specs=[row(GLA_KEY_WIDTH), row(GLA_KEY_WIDTH), row(GLA_KEY_WIDTH), row(GLA_WIDTH),
                  row(GLA_KEY_WIDTH), row(GLA_WIDTH), _const_spec((1, GLA_WIDTH)),
                  _const_spec((GLA_WIDTH, GLA_WIDTH))],
        out_specs=[row(GLA_WIDTH), pl.BlockSpec((None, GLA_WIDTH, GLA_KEY_WIDTH), lambda b, i: (b, 0, 0))],
        out_shape=[jax.ShapeDtypeStruct((t, GLA_WIDTH), BF16),
                   jax.ShapeDtypeStruct((n, GLA_WIDTH, GLA_KEY_WIDTH), F32)],
        scratch_shapes=[pltpu.VMEM((GLA_WIDTH, GLA_KEY_WIDTH), F32), pltpu.VMEM((tc, GLA_WIDTH), F32)],
        compiler_params=_params("parallel", "arbitrary"),
        name="gla_prompt",
    )(qt, kt, kd, gv, eb, sg, gn_row, hsum)


def _sample_sweep_step(s, is_last, q_ref, kown_ref, vown_ref, bias_ref, u_ref, k_refs, v_refs, o_ref,
                       qb_ref, acc_ref, carry_ref, *, past):
    pages = len(k_refs)
    bias = bias_ref[...]

    @pl.when(s == 0)
    def _():
        q = q_ref[...]
        qb_ref[...] = jnp.broadcast_to(q, qb_ref.shape)
        key_pos = past + 0 * lax.broadcasted_iota(jnp.int32, (SB_HEADS, 1), 0)
        valid = key_pos < past
        z = jnp.sum(q * kown_ref[...], axis=1) + bias
        sp = jnp.where(valid, _softplus(z), 0.0)
        a = jnp.where(valid, jnp.exp(z - sp), 0.0)
        lane = lax.broadcasted_iota(jnp.int32, acc_ref.shape, 2)
        own = jnp.broadcast_to(a[:, :, None] * vown_ref[...], acc_ref.shape)
        acc_ref[...] = jnp.where(lane == 0, own, 0.0)
        carry_ref[...] = -sp

    order = list(reversed(range(pages)))
    qb = qb_ref[...]
    z = jnp.concatenate([jnp.sum(k_refs[p][...] * qb, axis=1) for p in order], axis=0)
    z = z + jnp.concatenate([bias] * pages, axis=0)
    sp = _softplus(z)
    hi = sp.astype(BF16).astype(F32)
    lo = (sp - hi).astype(BF16).astype(F32)
    later = _dot(hi, u_ref[...]) + _dot(lo, u_ref[...])
    totals = jnp.sum(sp, axis=1, keepdims=True)
    carries = [carry_ref[...]]
    for i in range(pages):
        carries.append(carries[-1] - totals[i * SB_HEADS:(i + 1) * SB_HEADS])
    carry_ref[...] = carries[-1]
    a = jnp.exp(z - sp - later + jnp.concatenate(carries[:-1], axis=0))
    for h in range(SB_HEADS):
        acc = acc_ref[h]
        for i, p in enumerate(order):
            acc = acc + a[i * SB_HEADS + h:i * SB_HEADS + h + 1, :] * v_refs[p][h]
        acc_ref[h] = acc
    if is_last is not None:
        _sample_finish(is_last, o_ref, acc_ref)


def _sample_finish(is_last, o_ref, acc_ref):
    @pl.when(is_last)
    def _():
        o_ref[...] = jnp.sum(acc_ref[...], axis=2, keepdims=True)


N_SAMPLE_IN = 5


def _sample_scratch():
    return [pltpu.VMEM((SB_HEADS, HEAD_DIM, PAGE_SIZE), F32), pltpu.VMEM((SB_HEADS, HEAD_DIM, PAGE_SIZE), F32),
            pltpu.VMEM((SB_HEADS, 1), F32)]


def _sb_sample_kernel(pt_ref, *refs, pages, past):
    ins, refs = refs[:N_SAMPLE_IN], refs[N_SAMPLE_IN:]
    k_refs, v_refs = refs[:pages], refs[pages:2 * pages]
    o_ref, qb_ref, acc_ref, carry_ref = refs[2 * pages:]
    s = pl.program_id(1)
    _sample_sweep_step(s, s == pl.num_programs(1) - 1, *ins, k_refs, v_refs, o_ref, qb_ref, acc_ref, carry_ref,
                       past=past)


def _sb_sample(page_table, q, k_own, v_own, bias_col, u, cache_kt, cache_vt, *, layer, pages):
    nb, n_pages = page_table.shape
    n_steps = n_pages // pages
    past = n_pages * PAGE_SIZE
    page_blk = (None, None, SB_HEADS, HEAD_DIM, PAGE_SIZE)

    def page_spec(p):
        return pl.BlockSpec(page_blk, lambda b, s, pt: (layer, pt[b, (n_steps - 1 - s) * pages + p], 0, 0, 0))

    per_b = pl.BlockSpec((None, SB_HEADS, HEAD_DIM, 1), lambda b, s, pt: (b, 0, 0, 0))
    grid_spec = pltpu.PrefetchScalarGridSpec(
        num_scalar_prefetch=1,
        grid=(nb, n_steps),
        in_specs=[per_b, per_b, per_b,
                  pl.BlockSpec((SB_HEADS, 1), lambda b, s, pt: (0, 0)),
                  pl.BlockSpec((PAGE_SIZE, PAGE_SIZE), lambda b, s, pt: (0, 0))]
                 + [page_spec(p) for p in range(pages)] * 2,
        out_specs=per_b,
        scratch_shapes=_sample_scratch(),
    )
    return pl.pallas_call(
        functools.partial(_sb_sample_kernel, pages=pages, past=past),
        grid_spec=grid_spec,
        out_shape=jax.ShapeDtypeStruct((nb, SB_HEADS, HEAD_DIM, 1), F32),
        compiler_params=_params("parallel", "arbitrary"),
        name="sb_sample",
    )(page_table, q, k_own, v_own, bias_col, u, *([cache_kt] * pages), *([cache_vt] * pages))


def _gla_sample_kernel(a_ref, s_ref, k_ref, v_ref, q_ref, sg_ref, gn_ref, s_out, og_out):
    s_new = a_ref[...] * s_ref[...] + k_ref[...] * v_ref[...]
    s_out[...] = s_new
    o = jnp.sum(q_ref[...] * s_new, axis=2)
    og_out[...] = _rms(o, gn_ref[...]) * sg_ref[...]


def _gla_sample(a, s0, gk, gv, gq, sg, gn):
    nb = s0.shape[0]
    col = lambda t: t.reshape(nb, GLA_HEADS, GLA_DK, 1)
    args = (col(a), s0, col(gk), gv.reshape(nb, GLA_HEADS, 1, GLA_DV), col(gq),
            sg.reshape(nb, GLA_HEADS, GLA_DV), gn.reshape(1, 1, GLA_DV))
    return pl.pallas_call(
        _gla_sample_kernel,
        out_shape=[jax.ShapeDtypeStruct(s0.shape, F32), jax.ShapeDtypeStruct((nb, GLA_HEADS, GLA_DV), F32)],
        name="gla_sample",
    )(*args)


def _tile(t, target):
    return target if t % target == 0 else t


def kernel(x_prompt, x_sample, cache_k, cache_v, state_gla, page_table, ffn1_norm_g, ffn1_w_up, ffn1_w_down, mix_norm_g, w_in, q_norm_g, k_norm_g, sb_logit_bias, gla_w_gate2, gla_b_gate, gla_out_norm_g, cm_ln_g, cm_ln_b, cm_w_spatial, cm_b_spatial, w_out, ffn2_norm_g, ffn2_w_up, ffn2_w_down):
    nb_p, L, d = x_prompt.shape
    nb_s, n_dec, _ = x_sample.shape
    depth = w_in.shape[0]
    assert n_dec == 1 and d == D_MODEL and L % CM_CHUNK == 0
    tp = nb_p * L
    tm = _tile(L, 512)
    tq = _tile(L, 256)
    tf = D_FF // 2
    hosted_pages = _sb_hosted_pages(nb_p, L, tq, page_table)
    pages = hosted_pages or (8 if page_table.shape[1] % 8 == 0 else 1)

    idx = jnp.arange(tm)
    same_chunk = (idx[:, None] // GLA_CHUNK) == (idx[None, :] // GLA_CHUNK)
    cum = (same_chunk & (idx[None, :] <= idx[:, None])).astype(BF16)
    tot = same_chunk.astype(BF16)
    hsum512 = ((jnp.arange(SB_WIDTH)[:, None] // HEAD_DIM) == (jnp.arange(SB_WIDTH)[None, :] // HEAD_DIM)).astype(BF16)
    hsum256 = hsum512[:GLA_WIDTH, :GLA_WIDTH]
    iq = jnp.arange(tq)
    u_later = (iq[:, None] > iq[None, :]).astype(BF16)
    ip = jnp.arange(PAGE_SIZE)
    u_page = (ip[:, None] > ip[None, :]).astype(F32)
    cache_kt = jnp.transpose(cache_k, (0, 1, 3, 4, 2))
    cache_vt = jnp.transpose(cache_v, (0, 1, 3, 4, 2))

    xp = x_prompt.reshape(tp, d)
    xs = x_sample.reshape(nb_s * n_dec, d)
    row = lambda v: v.reshape(1, -1).astype(F32)
    outs = {k: [] for k in ("sp", "ks", "vs", "ss", "cv")}
    kv_stack = ()
    for l in range(depth):
        w = w_in[l]
        w_r = jnp.concatenate([w[:, :C_GG], w[:, C_GG + GLA_RANK:], w[:, C_GG:C_GG + GLA_RANK],
                               jnp.zeros((d, N_IN_PAD - w.shape[1]), w.dtype)], axis=1).astype(BF16)
        wg2 = jnp.concatenate([gla_w_gate2[l], jnp.zeros((LANES - GLA_RANK, GLA_KEY_WIDTH), F32)], axis=0)
        mw = (row(mix_norm_g[l]), w_r, row(jnp.tile(q_norm_g[l], SB_HEADS)), row(jnp.tile(k_norm_g[l], SB_HEADS)),
              wg2, row(gla_b_gate[l]), row(cm_ln_g[l]), row(cm_ln_b[l]), hsum512)
        f1 = (row(ffn1_norm_g[l]), ffn1_w_up[l].astype(BF16), ffn1_w_down[l].astype(BF16))
        f2 = (row(ffn2_norm_g[l]), ffn2_w_up[l].astype(BF16), ffn2_w_down[l].astype(BF16))
        wo = w_out[l].astype(BF16)
        gn_row = row(jnp.tile(gla_out_norm_g[l], GLA_HEADS))
        bs_rows = jnp.repeat(cm_b_spatial[l].T, CM_GROUP_DIM, axis=1)

        xp = _ffn(xp, *f1, tm=tm, tf=tf)
        (q, kf, vf, kb, vb, qt, kt, kd, gv, eb, sg, ocm) = _mix_prompt(xp, mw, cum, tot, cm_w_spatial[l], bs_rows, tm=tm,
                                                                         n=nb_p, L=L, prev_kv=kv_stack)
        kv_stack = (kf, vf)
        ts = xs.shape[0]
        xs = _ffn(xs, *f1, tm=ts, tf=tf)
        w00_row = row(jnp.repeat(cm_w_spatial[l][:, 0, 0], CM_GROUP_DIM))
        b0_row = row(jnp.repeat(cm_b_spatial[l][:, 0], CM_GROUP_DIM))
        (q_s, k_s, v_s, gq_s, gk_s, gv_s, a_s, sg_s, cv_s, ocm_s) = _mix_sample(xs, mw, w00_row, b0_row)
        col4 = lambda t: t.reshape(nb_s, SB_HEADS, HEAD_DIM, 1)
        sample = (page_table, col4(q_s), col4(k_s), col4(v_s), sb_logit_bias[l].reshape(SB_HEADS, 1), u_page,
                  cache_kt, cache_vt, l, pages)

        r3 = lambda t: t.reshape(nb_p, L, SB_WIDTH)
        if hosted_pages:
            o_sb, o_sb_s = _sb_prompt(r3(q), r3(kb), r3(vb), sb_logit_bias[l], u_later, tq=tq, sample=sample)
        else:
            o_sb = _sb_prompt(r3(q), r3(kb), r3(vb), sb_logit_bias[l], u_later, tq=tq)
            o_sb_s = _sb_sample(*sample[:-2], layer=l, pages=pages)
        o_sb = o_sb.reshape(tp, SB_WIDTH)
        o_gla, st = _gla_prompt(qt, kt, kd, gv, eb, sg, gn_row, hsum256, n=nb_p, L=L, tc=tm)
        xp = _ffn(xp, *f2, tm=tm, tf=tf, merge=(o_sb, o_gla, ocm, wo))
        st5 = st.reshape(nb_p, GLA_HEADS, GLA_DV, GLA_HEADS, GLA_DK)
        outs["sp"].append(jnp.stack([st5[:, h, :, h, :] for h in range(GLA_HEADS)], axis=1).transpose(0, 1, 3, 2))

        s_new, og = _gla_sample(a_s, state_gla[l], gk_s, gv_s, gq_s, sg_s, gla_out_norm_g[l])
        xs = _ffn(xs, *f2, tm=ts, tf=tf,
                  merge=(o_sb_s.reshape(ts, SB_WIDTH).astype(BF16), og.reshape(ts, GLA_WIDTH).astype(BF16), ocm_s, wo))
        outs["ks"].append(k_s.reshape(nb_s, n_dec, SB_HEADS, HEAD_DIM))
        outs["vs"].append(v_s.reshape(nb_s, n_dec, SB_HEADS, HEAD_DIM))
        outs["ss"].append(s_new)
        outs["cv"].append(cv_s.reshape(nb_s, n_dec, CM_GROUPS, CM_GROUP_DIM))

    st = lambda k: jnp.stack(outs[k])
    to_cache = lambda t: t.reshape(depth, nb_p, SB_HEADS, HEAD_DIM, L).transpose(0, 1, 4, 2, 3)
    return (xp.reshape(nb_p, L, d), xs.reshape(nb_s, n_dec, d), to_cache(kv_stack[0]), to_cache(kv_stack[1]),
            st("sp"), st("ks"), st("vs"), st("ss"), st("cv"))
```

```python
import functools

import jax
import jax.numpy as jnp
from jax import lax
from jax.experimental import pallas as pl
from jax.experimental.pallas import tpu as pltpu

F32 = jnp.float32
BF16 = jnp.bfloat16

D_MODEL = 1024
HEAD_DIM = 64
SB_HEADS = 8
SB_WIDTH = SB_HEADS * HEAD_DIM
GLA_HEADS = 4
GLA_DV = 64
GLA_DK = 32
GLA_WIDTH = GLA_HEADS * GLA_DV
GLA_KEY_WIDTH = GLA_HEADS * GLA_DK
GLA_RANK = 16
GLA_TAU = 16.0
GLA_CHUNK = 64
CM_GROUPS = 4
CM_WIDTH = 256
CM_GROUP_DIM = 64
CM_CHUNK = 128
D_FF = 2816
EPS = 1e-6
PAGE_SIZE = 128

LOG2E = 1.4426950408889634
MASKED_LOG = -1e30
UNROLL = 4
SB_STEP_HEADS = 2
LANES = 128
VMEM_LIMIT = 56 * 1024 * 1024

C_SQ, C_SK, C_SV = 0, 512, 1024
C_GQ, C_GK, C_GV, C_GG = 1536, 1664, 1792, 2048
C_CU, C_CV, C_LR = 2304, 2560, 2816
N_IN_PAD = 2944

NT = (((1,), (1,)), ((), ()))
TN = (((0,), (0,)), ((), ()))


def _params(*sem):
    return pltpu.CompilerParams(dimension_semantics=sem, vmem_limit_bytes=VMEM_LIMIT)


def _const_spec(shape):
    n = len(shape)
    return pl.BlockSpec(shape, lambda *_: (0,) * n)


def _softplus(z):
    return jnp.maximum(z, 0.0) + jnp.log1p(jnp.exp(-jnp.abs(z)))


def _softplus2(x):
    neg_abs = pltpu.bitcast(pltpu.bitcast(x, jnp.uint32) | jnp.uint32(0x80000000), F32)
    return jnp.maximum(x, 0.0) + jnp.log2(1.0 + jnp.exp2(neg_abs))


def _split_bf16(x):
    hi = x.astype(BF16)
    lo = (x - hi.astype(F32)).astype(BF16)
    return hi, lo


def _dot(a, b):
    return jnp.dot(a, b, preferred_element_type=F32)


def _rms(x, g_row):
    return x * lax.rsqrt(jnp.mean(x * x, axis=-1, keepdims=True) + EPS) * g_row


def _ffn_kernel(*refs, nf, merged):
    if merged:
        x_ref, sb_ref, gla_ref, cm_ref, wo_ref, g_ref, wg_ref, wu_ref, wd_ref, o_ref, h_ref, acc_ref, x1_ref = refs
    else:
        x_ref, g_ref, wg_ref, wu_ref, wd_ref, o_ref, h_ref, acc_ref = refs
        x1_ref = x_ref
    f = pl.program_id(1)

    @pl.when(f == 0)
    def _():
        if merged:
            o = _dot(sb_ref[...], wo_ref[0:SB_WIDTH, :])
            o = o + _dot(gla_ref[...], wo_ref[SB_WIDTH:SB_WIDTH + GLA_WIDTH, :])
            o = o + _dot(cm_ref[...], wo_ref[SB_WIDTH + GLA_WIDTH:, :])
            x1_ref[...] = x_ref[...] + o
        h_ref[...] = _rms(x1_ref[...], g_ref[...]).astype(BF16)
        acc_ref[...] = jnp.zeros_like(acc_ref)

    h = h_ref[...]
    gate = _dot(h, wg_ref[...])
    up = _dot(h, wu_ref[...])
    act = (gate * jax.nn.sigmoid(gate) * up).astype(BF16)
    acc_ref[...] += _dot(act, wd_ref[...])

    @pl.when(f == nf - 1)
    def _():
        o_ref[...] = x1_ref[...] + 0.5 * acc_ref[...]


def _ffn(x, g_row, w_up, w_down, *, tm, tf, merge=None):
    t, d = x.shape
    nf = D_FF // tf
    row = lambda w: pl.BlockSpec((tm, w), lambda i, f: (i, 0))
    merge_specs, merge_args, scratch = [], (), []
    if merge is not None:
        merge_specs = [row(SB_WIDTH), row(GLA_WIDTH), row(CM_WIDTH), _const_spec((d, d))]
        merge_args = tuple(merge)
        scratch = [pltpu.VMEM((tm, d), F32)]
    return pl.pallas_call(
        functools.partial(_ffn_kernel, nf=nf, merged=merge is not None),
        grid=(t // tm, nf),
        in_specs=[row(d)] + merge_specs + [
            _const_spec((1, d)),
            pl.BlockSpec((d, tf), lambda i, f: (0, f)),
            pl.BlockSpec((d, tf), lambda i, f: (0, f + nf)),
            pl.BlockSpec((tf, d), lambda i, f: (f, 0)),
        ],
        out_specs=row(d),
        out_shape=jax.ShapeDtypeStruct((t, d), F32),
        scratch_shapes=[pltpu.VMEM((tm, d), BF16), pltpu.VMEM((tm, d), F32)] + scratch,
        compiler_params=_params("parallel", "arbitrary"),
        name="half_ffn_merged" if merge is not None else "half_ffn",
    )(x, *merge_args, g_row, w_up, w_up, w_down)


def _mix_common(x_ref, g_ref, w_ref, qg_ref, kg_ref, wg2_ref, bg_ref, lng_ref, lnb_ref, hsum_ref):
    h = _rms(x_ref[...], g_ref[...]).astype(BF16)

    def proj(lo, width):
        return _dot(h, w_ref[:, lo:lo + width])

    def head_norm(t, gain_row):
        ms = _dot((t * t).astype(BF16), hsum_ref[...]) * (1.0 / HEAD_DIM)
        return t * lax.rsqrt(ms + EPS) * gain_row

    out = {}
    out["q"] = head_norm(proj(C_SQ, SB_WIDTH), qg_ref[...]) * (HEAD_DIM ** -0.5)
    out["k"] = head_norm(proj(C_SK, SB_WIDTH), kg_ref[...])
    out["v"] = proj(C_SV, SB_WIDTH)
    out["gq"] = proj(C_GQ, GLA_KEY_WIDTH) * (GLA_DK ** -0.5)
    out["gk"] = proj(C_GK, GLA_KEY_WIDTH)
    out["gv"] = proj(C_GV, GLA_WIDTH)
    gg = proj(C_GG, GLA_WIDTH)
    out["sg"] = gg * jax.nn.sigmoid(gg)
    lr_hi, lr_lo = _split_bf16(proj(C_LR, LANES))
    w_hi, w_lo = _split_bf16(wg2_ref[...])
    pre = _dot(lr_hi, w_hi) + _dot(lr_lo, w_hi) + _dot(lr_hi, w_lo) + bg_ref[...]
    out["la"] = -_softplus(-pre) * (1.0 / GLA_TAU)
    out["cu"] = jax.nn.gelu(proj(C_CU, CM_WIDTH))
    c = jax.nn.gelu(proj(C_CV, CM_WIDTH))
    c = c - jnp.mean(c, axis=-1, keepdims=True)
    out["cv"] = c * lax.rsqrt(jnp.mean(c * c, axis=-1, keepdims=True) + EPS) * lng_ref[...] + lnb_ref[...]
    return out


def _mix_prompt_kernel(x_ref, g_ref, w_ref, qg_ref, kg_ref, wg2_ref, bg_ref, lng_ref, lnb_ref, hsum_ref,
                       cum_ref, tot_ref, ws_ref, bs_ref, *refs, tm, n_prev):
    prev_k, prev_v = refs[:2] if n_prev else (None, None)
    q_o, kf_o, vf_o, kb_o, vb_o, qt_o, kt_o, kd_o, gv_o, eb_o, sg_o, ocm_o = refs[2 if n_prev else 0:]
    m = _mix_common(x_ref, g_ref, w_ref, qg_ref, kg_ref, wg2_ref, bg_ref, lng_ref, lnb_ref, hsum_ref)
    q_o[...] = (m["q"] * LOG2E).astype(BF16)
    if n_prev:
        kf_o[0:n_prev] = prev_k[...]
        vf_o[0:n_prev] = prev_v[...]
    kf_o[n_prev] = m["k"].T
    vf_o[n_prev] = m["v"].T
    kb_o[...] = m["k"].astype(BF16)
    vb_o[...] = m["v"].astype(BF16)
    gv_o[...] = m["gv"].astype(BF16)
    sg_o[...] = m["sg"]
    la_split = jnp.concatenate(_split_bf16(m["la"]), axis=1)
    b = _dot(cum_ref[...], la_split)
    b = b[:, :GLA_KEY_WIDTH] + b[:, GLA_KEY_WIDTH:]
    b_last = jnp.concatenate(
        [jnp.broadcast_to(b[c * GLA_CHUNK + GLA_CHUNK - 1:(c + 1) * GLA_CHUNK, :], (GLA_CHUNK, GLA_KEY_WIDTH))
         for c in range(tm // GLA_CHUNK)], axis=0)
    eb = jnp.exp(b)
    eb_o[...] = eb
    qt_o[...] = (m["gq"] * eb).astype(BF16)
    kt_o[...] = (m["gk"] * jnp.exp(-b)).astype(BF16)
    kd_o[...] = (m["gk"] * jnp.exp(b_last - b)).astype(BF16)
    row = lax.broadcasted_iota(jnp.int32, (CM_CHUNK, CM_CHUNK), 0)
    col = lax.broadcasted_iota(jnp.int32, (CM_CHUNK, CM_CHUNK), 1)
    group = lax.broadcasted_iota(jnp.int32, (1, CM_WIDTH), 1) // CM_GROUP_DIM
    w_tril = [jnp.where(row >= col, ws_ref[g], 0.0).astype(BF16) for g in range(CM_GROUPS)]
    cv = m["cv"].astype(BF16)
    for c in range(tm // CM_CHUNK):
        rows = slice(c * CM_CHUNK, (c + 1) * CM_CHUNK)
        mixed = bs_ref[...]
        for g in range(CM_GROUPS):
            mixed = mixed + _dot(w_tril[g], jnp.where(group == g, cv[rows], jnp.zeros_like(cv[rows])))
        ocm_o[rows, :] = (m["cu"][rows] * mixed).astype(BF16)


def _mix_sample_kernel(x_ref, g_ref, w_ref, qg_ref, kg_ref, wg2_ref, bg_ref, lng_ref, lnb_ref, hsum_ref,
                       w00_ref, b0_ref,
                       q_o, k_o, v_o, gq_o, gk_o, gv_o, a_o, sg_o, cv_o, ocm_o):
    m = _mix_common(x_ref, g_ref, w_ref, qg_ref, kg_ref, wg2_ref, bg_ref, lng_ref, lnb_ref, hsum_ref)
    q_o[...] = m["q"]
    k_o[...] = m["k"]
    v_o[...] = m["v"]
    gq_o[...] = m["gq"]
    gk_o[...] = m["gk"]
    gv_o[...] = m["gv"]
    a_o[...] = jnp.exp(m["la"])
    sg_o[...] = m["sg"]
    cv_o[...] = m["cv"]
    ocm_o[...] = (m["cu"] * (w00_ref[...] * m["cv"] + b0_ref[...])).astype(BF16)


def _mix_weight_specs():
    return [
        _const_spec((1, D_MODEL)),
        _const_spec((D_MODEL, N_IN_PAD)),
        _const_spec((1, SB_WIDTH)),
        _const_spec((1, SB_WIDTH)),
        _const_spec((LANES, GLA_KEY_WIDTH)),
        _const_spec((1, GLA_KEY_WIDTH)),
        _const_spec((1, CM_WIDTH)),
        _const_spec((1, CM_WIDTH)),
        _const_spec((SB_WIDTH, SB_WIDTH)),
    ]


def _mix_prompt(x, mw, cum, tot, w_s, bs_rows, *, tm, n, L, prev_kv=()):
    t = x.shape[0]
    nt = L // tm
    n_prev = prev_kv[0].shape[0] if prev_kv else 0
    kv_block = lambda layers: pl.BlockSpec((layers, None, SB_WIDTH, tm), lambda i: (0, i // nt, 0, i % nt))
    kv_spec = kv_block(n_prev + 1)
    kv_sds = jax.ShapeDtypeStruct((n_prev + 1, n, SB_WIDTH, L), F32)
    row = lambda w: pl.BlockSpec((tm, w), lambda i: (i, 0))
    sds = lambda w, dt: jax.ShapeDtypeStruct((t, w), dt)
    widths = [(SB_WIDTH, BF16), (SB_WIDTH, F32), (SB_WIDTH, F32), (SB_WIDTH, BF16), (SB_WIDTH, BF16),
              (GLA_KEY_WIDTH, BF16), (GLA_KEY_WIDTH, BF16), (GLA_KEY_WIDTH, BF16), (GLA_WIDTH, BF16),
              (GLA_KEY_WIDTH, F32), (GLA_WIDTH, F32), (CM_WIDTH, BF16)]
    return pl.pallas_call(
        functools.partial(_mix_prompt_kernel, tm=tm, n_prev=n_prev),
        grid=(t // tm,),
        in_specs=[row(D_MODEL)] + _mix_weight_specs() + [
            _const_spec((tm, tm)), _const_spec((tm, tm)),
            _const_spec((CM_GROUPS, CM_CHUNK, CM_CHUNK)), _const_spec((CM_CHUNK, CM_WIDTH))]
                 + [kv_block(n_prev)] * len(prev_kv),
        out_specs=[kv_spec if i in (1, 2) else row(w) for i, (w, _) in enumerate(widths)],
        out_shape=[kv_sds if i in (1, 2) else sds(w, dt) for i, (w, dt) in enumerate(widths)],
        compiler_params=_params("parallel"),
        name="mix_prompt",
    )(x, *mw, cum, tot, w_s, bs_rows, *prev_kv)


def _mix_sample(x, mw, w00_row, b0_row):
    t = x.shape[0]
    full = lambda w: _const_spec((t, w))
    widths = [(SB_WIDTH, F32), (SB_WIDTH, F32), (SB_WIDTH, F32), (GLA_KEY_WIDTH, F32), (GLA_KEY_WIDTH, F32),
              (GLA_WIDTH, F32), (GLA_KEY_WIDTH, F32), (GLA_WIDTH, F32), (CM_WIDTH, F32), (CM_WIDTH, BF16)]
    return pl.pallas_call(
        _mix_sample_kernel,
        grid=(1,),
        in_specs=[full(D_MODEL)] + _mix_weight_specs() + [_const_spec((1, CM_WIDTH)), _const_spec((1, CM_WIDTH))],
        out_specs=[full(w) for w, _ in widths],
        out_shape=[jax.ShapeDtypeStruct((t, w), dt) for w, dt in widths],
        compiler_params=_params("arbitrary"),
        name="mix_sample",
    )(x, *mw, w00_row, b0_row)


def _sb_prompt_kernel(*refs, tq, hosted):
    hp = pl.program_id(1)
    qi = pl.program_id(2)
    if hosted is None:
        bias_ref, q_ref, k_ref, v_ref, u_ref, o_ref, acc_ref, z_ref, w_ref, c_ref, r_ref = refs
    else:
        pages, steps_per_row, past = hosted
        bias_ref, q_ref, k_ref, v_ref, u_ref = refs[1:6]
        s_in = refs[6:6 + N_SAMPLE_IN]
        pages_at = 6 + N_SAMPLE_IN
        s_k, s_v = refs[pages_at:pages_at + pages], refs[pages_at + pages:pages_at + 2 * pages]
        o_ref, s_o, acc_ref, z_ref, w_ref, c_ref, r_ref, s_qb, s_acc, s_carry = refs[pages_at + 2 * pages:]
        s_step = qi % steps_per_row
        _sample_sweep_step(s_step, None, *s_in, s_k, s_v, s_o, s_qb, s_acc, s_carry, past=past)
    nh = q_ref.shape[-1] // HEAD_DIM
    lane_head = lax.broadcasted_iota(jnp.int32, (1, LANES), 1) // HEAD_DIM
    pair_lanes = [slice((hh // 2) * LANES, (hh // 2 + 1) * LANES) for hh in range(nh)]
    lane = lax.broadcasted_iota(jnp.int32, (1, LANES), 1)
    q_heads, k_fill = [], []
    for hh in range(nh):
        free = (1 - hh % 2) * HEAD_DIM
        rest = jnp.full((1, LANES), bias_ref[nh * hp + hh] * LOG2E, F32)
        parts = jnp.zeros((1, LANES), F32)
        for i in range(3):
            part = rest.astype(BF16).astype(F32)
            parts = jnp.where(lane == free + i, part, parts)
            rest = rest - part
        fill = jnp.broadcast_to(parts, (tq, LANES)).astype(BF16)
        q_heads.append(jnp.where(lane_head == hh % 2, q_ref[:, pair_lanes[hh]], fill))
        ones = jnp.where(jnp.logical_and(lane >= free, lane < free + 3), 1.0, 0.0)
        k_fill.append(jnp.broadcast_to(ones, (tq, LANES)).astype(BF16))
    row = lax.broadcasted_iota(jnp.int32, (tq, tq), 0)
    col = lax.broadcasted_iota(jnp.int32, (tq, tq), 1)
    heads = [slice(hh * tq, (hh + 1) * tq) for hh in range(nh)]

    def scores(j):
        rows = pl.ds(pl.multiple_of(j * tq, tq), tq)
        return jnp.concatenate(
            [lax.dot_general(q_heads[hh], jnp.where(lane_head == hh % 2, k_ref[rows, pair_lanes[hh]], k_fill[hh]),
                             NT, preferred_element_type=F32) for hh in range(nh)], axis=0)

    def log_terms(p, diagonal):
        for hh in range(nh):
            z = z_ref[p, heads[hh], :]
            sp = _softplus2(z)
            if diagonal:
                sp = jnp.where(col < row, sp, 0.0)
            w = z - sp - _dot(sp.astype(BF16), u_ref[...])
            if diagonal:
                w = jnp.where(col < row, w, MASKED_LOG)
            w_ref[p, heads[hh], :] = w
            r_ref[heads[hh], :] = jnp.broadcast_to(jnp.sum(sp, axis=1, keepdims=True), (tq, LANES))

    def attend(j, p):
        rows = pl.ds(pl.multiple_of(j * tq, tq), tq)
        a = []
        for hh in range(nh):
            c = c_ref[p, heads[hh], :]
            a.append(jnp.exp2(w_ref[p, heads[hh], :] + jnp.concatenate([c] * (tq // LANES), axis=1)).astype(BF16))
        for hh in range(0, nh, 2):
            o2 = _dot(jnp.concatenate(a[hh:hh + 2], axis=0), v_ref[rows, pair_lanes[hh]])
            acc_ref[:, pair_lanes[hh]] += jnp.where(lane_head == 0, o2[:tq], o2[tq:])

    def step(n, p, with_attend=True):
        if with_attend:
            attend(qi - n + 2, p)
        c_ref[p] = c_ref[1 - p] - r_ref[...]
        z_ref[1 - p] = scores(jnp.maximum(qi - n - 1, 0))
        log_terms(p, False)

    acc_ref[...] = jnp.zeros_like(acc_ref)
    c_ref[0] = jnp.zeros(c_ref.shape[1:], F32)
    z_ref[0] = scores(qi)
    log_terms(0, True)
    z_ref[1] = scores(jnp.maximum(qi - 1, 0))

    @pl.when(qi >= 1)
    def _():
        step(1, 1, with_attend=False)

    left = jnp.maximum(qi - 1, 0)

    def body(i, _):
        for k in range(UNROLL):
            step(2 + UNROLL * i + k, k % 2)
        return 0

    lax.fori_loop(0, left // UNROLL, body, 0)
    nxt = 2 + (left // UNROLL) * UNROLL
    rem = left % UNROLL

    @pl.when(rem >= 2)
    def _():
        step(nxt, 0)
        step(nxt + 1, 1)

    @pl.when(rem % 2 == 1)
    def _():
        step(qi, 0)

    @pl.when(qi >= 1)
    def _():
        attend(1, (qi - 1) % 2)

    attend(0, qi % 2)
    o_ref[...] = acc_ref[...].astype(BF16)
    if hosted is not None:
        _sample_finish(s_step == steps_per_row - 1, s_o, s_acc)


def _sb_hosted_pages(n, L, tq, page_table):
    nb, n_pages = page_table.shape
    steps = n * (SB_HEADS // SB_STEP_HEADS) * (L // tq)
    if steps % nb:
        return None
    steps_per_row = steps // nb
    if n_pages % steps_per_row or (L // tq) % steps_per_row:
        return None
    return n_pages // steps_per_row


def _sb_prompt(q, k, v, bias, u, *, tq, sample=None):
    n, L, _ = q.shape
    nh, nq = SB_HEADS // SB_STEP_HEADS, L // tq
    width, rows_z = SB_STEP_HEADS * HEAD_DIM, SB_STEP_HEADS * tq
    blk = lambda rows: (None, rows, width)
    in_specs = [
        pl.BlockSpec(memory_space=pltpu.SMEM),
        pl.BlockSpec(blk(tq), lambda b, h, i, *_: (b, i, h)),
        pl.BlockSpec(blk(L), lambda b, h, i, *_: (b, 0, h)),
        pl.BlockSpec(blk(L), lambda b, h, i, *_: (b, 0, h)),
        pl.BlockSpec((tq, tq), lambda b, h, i, *_: (0, 0)),
    ]
    out_specs = pl.BlockSpec(blk(tq), lambda b, h, i, *_: (b, i, h))
    out_shape = jax.ShapeDtypeStruct((n, L, SB_WIDTH), BF16)
    scratch = [pltpu.VMEM((tq, width), F32), pltpu.VMEM((2, rows_z, tq), F32), pltpu.VMEM((2, rows_z, tq), F32),
               pltpu.VMEM((2, rows_z, LANES), F32), pltpu.VMEM((rows_z, LANES), F32)]
    if sample is None:
        return pl.pallas_call(
            functools.partial(_sb_prompt_kernel, tq=tq, hosted=None),
            grid=(n, nh, nq), in_specs=in_specs, out_specs=out_specs, out_shape=out_shape, scratch_shapes=scratch,
            compiler_params=_params("parallel", "parallel", "arbitrary"),
            name="sb_prompt",
        )(bias, q, k, v, u)

    page_table, s_q, s_k, s_v, s_bias, s_u, cache_kt, cache_vt, layer, pages = sample
    nb, n_pages = page_table.shape
    spr = n_pages // pages
    row_of = lambda b, h, i: (b * nh + h) * (nq // spr) + i // spr

    def page_spec(p):
        return pl.BlockSpec(
            (None, None, SB_HEADS, HEAD_DIM, PAGE_SIZE),
            lambda b, h, i, pt: (layer, pt[row_of(b, h, i), (spr - 1 - i % spr) * pages + p], 0, 0, 0))

    per_row = pl.BlockSpec((None, SB_HEADS, HEAD_DIM, 1), lambda b, h, i, pt: (row_of(b, h, i), 0, 0, 0))
    grid_spec = pltpu.PrefetchScalarGridSpec(
        num_scalar_prefetch=1,
        grid=(n, nh, nq),
        in_specs=in_specs + [per_row, per_row, per_row,
                             pl.BlockSpec((SB_HEADS, 1), lambda b, h, i, pt: (0, 0)),
                             pl.BlockSpec((PAGE_SIZE, PAGE_SIZE), lambda b, h, i, pt: (0, 0))]
                 + [page_spec(p) for p in range(pages)] * 2,
        out_specs=[out_specs, per_row],
        scratch_shapes=scratch + _sample_scratch(),
    )
    return pl.pallas_call(
        functools.partial(_sb_prompt_kernel, tq=tq, hosted=(pages, spr, n_pages * PAGE_SIZE)),
        grid_spec=grid_spec,
        out_shape=[out_shape, jax.ShapeDtypeStruct((nb, SB_HEADS, HEAD_DIM, 1), F32)],
        compiler_params=_params("arbitrary", "arbitrary", "arbitrary"),
        name="sb_prompt_hosting_sample",
    )(page_table, bias, q, k, v, u, s_q, s_k, s_v, s_bias, s_u, *([cache_kt] * pages), *([cache_vt] * pages))


def _gla_prompt_kernel(qt_ref, kt_ref, kd_ref, v_ref, eb_ref, sg_ref, gn_ref, hsum_ref,
                       og_ref, st_ref, s_ref, o_ref, *, tc):
    @pl.when(pl.program_id(1) == 0)
    def _():
        s_ref[...] = jnp.zeros_like(s_ref)

    ck = GLA_CHUNK
    key_head = lax.broadcasted_iota(jnp.int32, (1, GLA_KEY_WIDTH), 1) // GLA_DK
    val_head = lax.broadcasted_iota(jnp.int32, (1, GLA_WIDTH), 1) // GLA_DV
    state_mask = (lax.broadcasted_iota(jnp.int32, (GLA_WIDTH, GLA_KEY_WIDTH), 0) // GLA_DV
                  == lax.broadcasted_iota(jnp.int32, (GLA_WIDTH, GLA_KEY_WIDTH), 1) // GLA_DK)
    causal = (lax.broadcasted_iota(jnp.int32, (ck, ck), 0) >= lax.broadcasted_iota(jnp.int32, (ck, ck), 1))
    for c in range(tc // ck):
        rows = slice(c * ck, (c + 1) * ck)
        qt, kt, kd, v = qt_ref[rows, :], kt_ref[rows, :], kd_ref[rows, :], v_ref[rows, :]
        att = []
        v_bd = []
        for hh in range(GLA_HEADS):
            qh = jnp.where(key_head == hh, qt, jnp.zeros_like(qt))
            s = lax.dot_general(qh, kt, NT, preferred_element_type=F32)
            att.append(jnp.where(causal, s, 0.0).astype(BF16))
            v_bd.append(jnp.where(val_head == hh, v, jnp.zeros_like(v)))
        st = s_ref[...]
        o = _dot(jnp.concatenate(att, axis=1), jnp.concatenate(v_bd, axis=0))
        o = o + lax.dot_general(qt, st.astype(BF16), NT, preferred_element_type=F32)
        o_ref[rows, :] = o
        ds = lax.dot_general(v, kd, TN, preferred_element_type=F32)
        decay = eb_ref[c * ck + ck - 1:c * ck + ck, :]
        s_ref[...] = decay * st + jnp.where(state_mask, ds, 0.0)
    o = o_ref[...]
    ms = _dot((o * o).astype(BF16), hsum_ref[...]) * (1.0 / GLA_DV)
    og_ref[...] = (o * lax.rsqrt(ms + EPS) * gn_ref[...] * sg_ref[...]).astype(BF16)
    st_ref[...] = s_ref[...]


def _gla_prompt(qt, kt, kd, gv, eb, sg, gn_row, hsum, *, n, L, tc):
    t = n * L
    nc = L // tc
    row = lambda w: pl.BlockSpec((tc, w), lambda b, i: (b * nc + i, 0))
    return pl.pallas_call(
        functools.partial(_gla_prompt_kernel, tc=tc),
        grid=(n, nc),
        in_specs=[row(GLA_KEY_WIDTH), row(GLA_KEY_WIDTH), row(GLA_KEY_WIDTH), row(GLA_WIDTH),
                  row(GLA_KEY_WIDTH), row(GLA_WIDTH), _const_spec((1, GLA_WIDTH)),
                  _const_spec((GLA_WIDTH, GLA_WIDTH))],
        out_specs=[row(GLA_WIDTH), pl.BlockSpec((None, GLA_WIDTH, GLA_KEY_WIDTH), lambda b, i: (b, 0, 0))],
        out_shape=[jax.ShapeDtypeStruct((t, GLA_WIDTH), BF16),
                   jax.ShapeDtypeStruct((n, GLA_WIDTH, GLA_KEY_WIDTH), F32)],
        scratch_shapes=[pltpu.VMEM((GLA_WIDTH, GLA_KEY_WIDTH), F32), pltpu.VMEM((tc, GLA_WIDTH), F32)],
        compiler_params=_params("parallel", "arbitrary"),
        name="gla_prompt",
    )(qt, kt, kd, gv, eb, sg, gn_row, hsum)


def _sample_sweep_step(s, is_last, q_ref, kown_ref, vown_ref, bias_ref, u_ref, k_refs, v_refs, o_ref,
                       qb_ref, acc_ref, carry_ref, *, past):
    pages = len(k_refs)
    bias = bias_ref[...]

    @pl.when(s == 0)
    def _():
        q = q_ref[...]
        qb_ref[...] = jnp.broadcast_to(q, qb_ref.shape)
        key_pos = past + 0 * lax.broadcasted_iota(jnp.int32, (SB_HEADS, 1), 0)
        valid = key_pos < past
        z = jnp.sum(q * kown_ref[...], axis=1) + bias
        sp = jnp.where(valid, _softplus(z), 0.0)
        a = jnp.where(valid, jnp.exp(z - sp), 0.0)
        lane = lax.broadcasted_iota(jnp.int32, acc_ref.shape, 2)
        own = jnp.broadcast_to(a[:, :, None] * vown_ref[...], acc_ref.shape)
        acc_ref[...] = jnp.where(lane == 0, own, 0.0)
        carry_ref[...] = -sp

    order = list(reversed(range(pages)))
    qb = qb_ref[...]
    z = jnp.concatenate([jnp.sum(k_refs[p][...] * qb, axis=1) for p in order], axis=0)
    z = z + jnp.concatenate([bias] * pages, axis=0)
    sp = _softplus(z)
    hi = sp.astype(BF16).astype(F32)
    lo = (sp - hi).astype(BF16).astype(F32)
    later = _dot(hi, u_ref[...]) + _dot(lo, u_ref[...])
    totals = jnp.sum(sp, axis=1, keepdims=True)
    carries = [carry_ref[...]]
    for i in range(pages):
        carries.append(carries[-1] - totals[i * SB_HEADS:(i + 1) * SB_HEADS])
    carry_ref[...] = carries[-1]
    a = jnp.exp(z - sp - later + jnp.concatenate(carries[:-1], axis=0))
    for h in range(SB_HEADS):
        acc = acc_ref[h]
        for i, p in enumerate(order):
            acc = acc + a[i * SB_HEADS + h:i * SB_HEADS + h + 1, :] * v_refs[p][h]
        acc_ref[h] = acc
    if is_last is not None:
        _sample_finish(is_last, o_ref, acc_ref)


def _sample_finish(is_last, o_ref, acc_ref):
    @pl.when(is_last)
    def _():
        o_ref[...] = jnp.sum(acc_ref[...], axis=2, keepdims=True)


N_SAMPLE_IN = 5


def _sample_scratch():
    return [pltpu.VMEM((SB_HEADS, HEAD_DIM, PAGE_SIZE), F32), pltpu.VMEM((SB_HEADS, HEAD_DIM, PAGE_SIZE), F32),
            pltpu.VMEM((SB_HEADS, 1), F32)]


def _sb_sample_kernel(pt_ref, *refs, pages, past):
    ins, refs = refs[:N_SAMPLE_IN], refs[N_SAMPLE_IN:]
    k_refs, v_refs = refs[:pages], refs[pages:2 * pages]
    o_ref, qb_ref, acc_ref, carry_ref = refs[2 * pages:]
    s = pl.program_id(1)
    _sample_sweep_step(s, s == pl.num_programs(1) - 1, *ins, k_refs, v_refs, o_ref, qb_ref, acc_ref, carry_ref,
                       past=past)


def _sb_sample(page_table, q, k_own, v_own, bias_col, u, cache_kt, cache_vt, *, layer, pages):
    nb, n_pages = page_table.shape
    n_steps = n_pages // pages
    past = n_pages * PAGE_SIZE
    page_blk = (None, None, SB_HEADS, HEAD_DIM, PAGE_SIZE)

    def page_spec(p):
        return pl.BlockSpec(page_blk, lambda b, s, pt: (layer, pt[b, (n_steps - 1 - s) * pages + p], 0, 0, 0))

    per_b = pl.BlockSpec((None, SB_HEADS, HEAD_DIM, 1), lambda b, s, pt: (b, 0, 0, 0))
    grid_spec = pltpu.PrefetchScalarGridSpec(
        num_scalar_prefetch=1,
        grid=(nb, n_steps),
        in_specs=[per_b, per_b, per_b,
                  pl.BlockSpec((SB_HEADS, 1), lambda b, s, pt: (0, 0)),
                  pl.BlockSpec((PAGE_SIZE, PAGE_SIZE), lambda b, s, pt: (0, 0))]
                 + [page_spec(p) for p in range(pages)] * 2,
        out_specs=per_b,
        scratch_shapes=_sample_scratch(),
    )
    return pl.pallas_call(
        functools.partial(_sb_sample_kernel, pages=pages, past=past),
        grid_spec=grid_spec,
        out_shape=jax.ShapeDtypeStruct((nb, SB_HEADS, HEAD_DIM, 1), F32),
        compiler_params=_params("parallel", "arbitrary"),
        name="sb_sample",
    )(page_table, q, k_own, v_own, bias_col, u, *([cache_kt] * pages), *([cache_vt] * pages))


def _gla_sample_kernel(a_ref, s_ref, k_ref, v_ref, q_ref, sg_ref, gn_ref, s_out, og_out):
    s_new = a_ref[...] * s_ref[...] + k_ref[...] * v_ref[...]
    s_out[...] = s_new
    o = jnp.sum(q_ref[...] * s_new, axis=2)
    og_out[...] = _rms(o, gn_ref[...]) * sg_ref[...]


def _gla_sample(a, s0, gk, gv, gq, sg, gn):
    nb = s0.shape[0]
    col = lambda t: t.reshape(nb, GLA_HEADS, GLA_DK, 1)
    args = (col(a), s0, col(gk), gv.reshape(nb, GLA_HEADS, 1, GLA_DV), col(gq),
            sg.reshape(nb, GLA_HEADS, GLA_DV), gn.reshape(1, 1, GLA_DV))
    return pl.pallas_call(
        _gla_sample_kernel,
        out_shape=[jax.ShapeDtypeStruct(s0.shape, F32), jax.ShapeDtypeStruct((nb, GLA_HEADS, GLA_DV), F32)],
        name="gla_sample",
    )(*args)


def _tile(t, target):
    return target if t % target == 0 else t


def kernel(x_prompt, x_sample, cache_k, cache_v, state_gla, page_table, ffn1_norm_g, ffn1_w_up, ffn1_w_down, mix_norm_g, w_in, q_norm_g, k_norm_g, sb_logit_bias, gla_w_gate2, gla_b_gate, gla_out_norm_g, cm_ln_g, cm_ln_b, cm_w_spatial, cm_b_spatial, w_out, ffn2_norm_g, ffn2_w_up, ffn2_w_down):
    nb_p, L, d = x_prompt.shape
    nb_s, n_dec, _ = x_sample.shape
    depth = w_in.shape[0]
    assert n_dec == 1 and d == D_MODEL and L % CM_CHUNK == 0
    tp = nb_p * L
    tm = _tile(L, 512)
    tq = _tile(L, 256)
    tf = D_FF // 2
    hosted_pages = _sb_hosted_pages(nb_p, L, tq, page_table)
    pages = hosted_pages or (8 if page_table.shape[1] % 8 == 0 else 1)

    idx = jnp.arange(tm)
    same_chunk = (idx[:, None] // GLA_CHUNK) == (idx[None, :] // GLA_CHUNK)
    cum = (same_chunk & (idx[None, :] <= idx[:, None])).astype(BF16)
    tot = same_chunk.astype(BF16)
    hsum512 = ((jnp.arange(SB_WIDTH)[:, None] // HEAD_DIM) == (jnp.arange(SB_WIDTH)[None, :] // HEAD_DIM)).astype(BF16)
    hsum256 = hsum512[:GLA_WIDTH, :GLA_WIDTH]
    iq = jnp.arange(tq)
    u_later = (iq[:, None] > iq[None, :]).astype(BF16)
    ip = jnp.arange(PAGE_SIZE)
    u_page = (ip[:, None] > ip[None, :]).astype(F32)
    cache_kt = jnp.transpose(cache_k, (0, 1, 3, 4, 2))
    cache_vt = jnp.transpose(cache_v, (0, 1, 3, 4, 2))

    xp = x_prompt.reshape(tp, d)
    xs = x_sample.reshape(nb_s * n_dec, d)
    row = lambda v: v.reshape(1, -1).astype(F32)
    outs = {k: [] for k in ("sp", "ks", "vs", "ss", "cv")}
    kv_stack = ()
    for l in range(depth):
        w = w_in[l]
        w_r = jnp.concatenate([w[:, :C_GG], w[:, C_GG + GLA_RANK:], w[:, C_GG:C_GG + GLA_RANK],
                               jnp.zeros((d, N_IN_PAD - w.shape[1]), w.dtype)], axis=1).astype(BF16)
        wg2 = jnp.concatenate([gla_w_gate2[l], jnp.zeros((LANES - GLA_RANK, GLA_KEY_WIDTH), F32)], axis=0)
        mw = (row(mix_norm_g[l]), w_r, row(jnp.tile(q_norm_g[l], SB_HEADS)), row(jnp.tile(k_norm_g[l], SB_HEADS)),
              wg2, row(gla_b_gate[l]), row(cm_ln_g[l]), row(cm_ln_b[l]), hsum512)
        f1 = (row(ffn1_norm_g[l]), ffn1_w_up[l].astype(BF16), ffn1_w_down[l].astype(BF16))
        f2 = (row(ffn2_norm_g[l]), ffn2_w_up[l].astype(BF16), ffn2_w_down[l].astype(BF16))
        wo = w_out[l].astype(BF16)
        gn_row = row(jnp.tile(gla_out_norm_g[l], GLA_HEADS))
        bs_rows = jnp.repeat(cm_b_spatial[l].T, CM_GROUP_DIM, axis=1)

        xp = _ffn(xp, *f1, tm=tm, tf=tf)
        (q, kf, vf, kb, vb, qt, kt, kd, gv, eb, sg, ocm) = _mix_prompt(xp, mw, cum, tot, cm_w_spatial[l], bs_rows, tm=tm,
                                                                         n=nb_p, L=L, prev_kv=kv_stack)
        kv_stack = (kf, vf)
        ts = xs.shape[0]
        xs = _ffn(xs, *f1, tm=ts, tf=tf)
        w00_row = row(jnp.repeat(cm_w_spatial[l][:, 0, 0], CM_GROUP_DIM))
        b0_row = row(jnp.repeat(cm_b_spatial[l][:, 0], CM_GROUP_DIM))
        (q_s, k_s, v_s, gq_s, gk_s, gv_s, a_s, sg_s, cv_s, ocm_s) = _mix_sample(xs, mw, w00_row, b0_row)
        col4 = lambda t: t.reshape(nb_s, SB_HEADS, HEAD_DIM, 1)
        sample = (page_table, col4(q_s), col4(k_s), col4(v_s), sb_logit_bias[l].reshape(SB_HEADS, 1), u_page,
                  cache_kt, cache_vt, l, pages)

        r3 = lambda t: t.reshape(nb_p, L, SB_WIDTH)
        if hosted_pages:
            o_sb, o_sb_s = _sb_prompt(r3(q), r3(kb), r3(vb), sb_logit_bias[l], u_later, tq=tq, sample=sample)
        else:
            o_sb = _sb_prompt(r3(q), r3(kb), r3(vb), sb_logit_bias[l], u_later, tq=tq)
            o_sb_s = _sb_sample(*sample[:-2], layer=l, pages=pages)
        o_sb = o_sb.reshape(tp, SB_WIDTH)
        o_gla, st = _gla_prompt(qt, kt, kd, gv, eb, sg, gn_row, hsum256, n=nb_p, L=L, tc=tm)
        xp = _ffn(xp, *f2, tm=tm, tf=tf, merge=(o_sb, o_gla, ocm, wo))
        st5 = st.reshape(nb_p, GLA_HEADS, GLA_DV, GLA_HEADS, GLA_DK)
        outs["sp"].append(jnp.stack([st5[:, h, :, h, :] for h in range(GLA_HEADS)], axis=1).transpose(0, 1, 3, 2))

        s_new, og = _gla_sample(a_s, state_gla[l], gk_s, gv_s, gq_s, sg_s, gla_out_norm_g[l])
        xs = _ffn(xs, *f2, tm=ts, tf=tf,
                  merge=(o_sb_s.reshape(ts, SB_WIDTH).astype(BF16), og.reshape(ts, GLA_WIDTH).astype(BF16), ocm_s, wo))
        outs["ks"].append(k_s.reshape(nb_s, n_dec, SB_HEADS, HEAD_DIM))
        outs["vs"].append(v_s.reshape(nb_s, n_dec, SB_HEADS, HEAD_DIM))
        outs["ss"].append(s_new)
        outs["cv"].append(cv_s.reshape(nb_s, n_dec, CM_GROUPS, CM_GROUP_DIM))

    st = lambda k: jnp.stack(outs[k])
    to_cache = lambda t: t.reshape(depth, nb_p, SB_HEADS, HEAD_DIM, L).transpose(0, 1, 4, 2, 3)
    return (xp.reshape(nb_p, L, d), xs.reshape(nb_s, n_dec, d), to_cache(kv_stack[0]), to_cache(kv_stack[1]),
            st("sp"), st("ks"), st("vs"), st("ss"), st("cv"))
```
